```python
import math, functools
import jax, jax.numpy as jnp
from jax import lax
import numpy as np

D_MODEL = 2048
BATCH = 4
SEQ = 2048
DEPTH = 4
DEC_BATCH = 8
DEC_SEQ = 4
PAST_LEN = 16384
PAGE_SIZE = 128

D_MIX = D_MODEL
DIFF_WIDTH = D_MIX // 2
FOX_WIDTH = D_MIX - DIFF_WIDTH
DIFF_HEAD_DIM = 128
DIFF_QK_DIM = DIFF_HEAD_DIM // 2
N_DIFF_HEADS = DIFF_WIDTH // DIFF_HEAD_DIM
FOX_HEAD_DIM = 128
N_FOX_HEADS = FOX_WIDTH // FOX_HEAD_DIM
D_IN = 4 * DIFF_WIDTH + 4 * FOX_WIDTH + N_FOX_HEADS
SPLIT_POINTS = [DIFF_WIDTH, 2 * DIFF_WIDTH, 3 * DIFF_WIDTH, 4 * DIFF_WIDTH,
                4 * DIFF_WIDTH + FOX_WIDTH, 4 * DIFF_WIDTH + 2 * FOX_WIDTH,
                4 * DIFF_WIDTH + 3 * FOX_WIDTH, 4 * DIFF_WIDTH + 4 * FOX_WIDTH]
N_BUCKETS = 32
MAX_DISTANCE = 128
Q_BLOCK = 128
EPS = 1e-6

kernel_name = "hybrid_diff_fox_decoder_step"


def rmsnorm(x, g):
    xf = x.astype(jnp.float32)
    y = xf * lax.rsqrt(jnp.mean(xf * xf, axis=-1, keepdims=True) + EPS)
    return (y * g.astype(jnp.float32)).astype(x.dtype)


def lambda_init(l):
    return 0.8 - 0.6 * math.exp(-0.3 * l)


def t5_bucket(q_pos, k_pos):
    n = jnp.maximum(q_pos[:, None] - k_pos[None, :], 0)
    max_exact = N_BUCKETS // 2
    large = max_exact + (jnp.log(jnp.maximum(n, 1).astype(jnp.float32) / max_exact)
                         / math.log(MAX_DISTANCE / max_exact) * (N_BUCKETS - max_exact)).astype(jnp.int32)
    large = jnp.minimum(large, N_BUCKETS - 1)
    return jnp.where(n < max_exact, n, large)


def diff_core(q1, q2, k1, k2, v, q_pos, k_pos, rel_bias, lam):
    scale = DIFF_QK_DIM ** -0.5
    mask = k_pos[None, :] <= q_pos[:, None]
    bias = jnp.moveaxis(rel_bias[t5_bucket(q_pos, k_pos)], -1, 0).astype(jnp.float32)

    def probs(q, k):
        s = jnp.einsum('bqhd,bkhd->bhqk', q, k).astype(jnp.float32) * scale + bias
        return jax.nn.softmax(jnp.where(mask, s, -jnp.inf), axis=-1)

    a = probs(q1, k1) - lam * probs(q2, k2)
    return jnp.einsum('bhqk,bkhd->bqhd', a.astype(v.dtype), v)


def fox_core(q, k, v, cq, ck, q_pos, k_pos):
    scale = FOX_HEAD_DIM ** -0.5
    mask = k_pos[None, :] <= q_pos[:, None]
    decay = jnp.moveaxis(cq, -1, 1)[..., :, None] - jnp.moveaxis(ck, -1, 1)[..., None, :]
    s = jnp.einsum('bqhd,bkhd->bhqk', q, k).astype(jnp.float32) * scale + decay
    p = jax.nn.softmax(jnp.where(mask, s, -jnp.inf), axis=-1)
    return jnp.einsum('bhqk,bkhd->bqhd', p.astype(v.dtype), v)


def project(h, w_in_l, b_f_l):
    B, T, _ = h.shape
    z = jnp.einsum('btd,de->bte', h, w_in_l)
    dq, dk, dv, dg, fq, fk, fv, fg, ff = jnp.split(z, SPLIT_POINTS, axis=-1)
    dq = dq.reshape(B, T, N_DIFF_HEADS, DIFF_HEAD_DIM)
    dk = dk.reshape(B, T, N_DIFF_HEADS, DIFF_HEAD_DIM)
    dv = dv.reshape(B, T, N_DIFF_HEADS, DIFF_HEAD_DIM)
    fq = fq.reshape(B, T, N_FOX_HEADS, FOX_HEAD_DIM)
    fk = fk.reshape(B, T, N_FOX_HEADS, FOX_HEAD_DIM)
    fv = fv.reshape(B, T, N_FOX_HEADS, FOX_HEAD_DIM)
    logf = jax.nn.log_sigmoid((ff + b_f_l).astype(jnp.float32))
    return dq, dk, dv, dg, fq, fk, fv, fg, logf


def prompt_mixers(dq, dk, dv, fq, fk, fv, logf, lam, rel_bias):
    B, T = dq.shape[:2]
    nb = T // Q_BLOCK
    pos = jnp.arange(T, dtype=jnp.int32)
    cum = jnp.cumsum(logf, axis=1)
    k1, k2 = dk[..., :DIFF_QK_DIM], dk[..., DIFF_QK_DIM:]

    def blocks(a):
        return jnp.moveaxis(a.reshape(B, nb, Q_BLOCK, *a.shape[2:]), 1, 0)

    def one_block(args):
        q_blk, qf_blk, cq_blk, q_pos = args
        od = diff_core(q_blk[..., :DIFF_QK_DIM], q_blk[..., DIFF_QK_DIM:], k1, k2, dv,
                       q_pos, pos, rel_bias, lam)
        of = fox_core(qf_blk, fk, fv, cq_blk, cum, q_pos, pos)
        return od, of

    od, of = lax.map(one_block, (blocks(dq), blocks(fq), blocks(cum), pos.reshape(nb, Q_BLOCK)))

    def unblocks(a):
        return jnp.moveaxis(a, 0, 1).reshape(B, T, *a.shape[3:])

    return unblocks(od), unblocks(of)


def sample_mixers(dq, dk, dv, fq, fk, fv, logf, lam, past_dk, past_dv, past_fk, past_fv, past_logf, rel_bias):
    P = past_dk.shape[1]
    T = dq.shape[1]
    k_all = jnp.concatenate([past_dk, dk], axis=1)
    v_all = jnp.concatenate([past_dv, dv], axis=1)
    fk_all = jnp.concatenate([past_fk, fk], axis=1)
    fv_all = jnp.concatenate([past_fv, fv], axis=1)
    cum = jnp.cumsum(jnp.concatenate([past_logf.astype(jnp.float32), logf], axis=1), axis=1)
    k_pos = jnp.arange(P + T, dtype=jnp.int32)
    q_pos = P + jnp.arange(T, dtype=jnp.int32)
    od = diff_core(dq[..., :DIFF_QK_DIM], dq[..., DIFF_QK_DIM:],
                   k_all[..., :DIFF_QK_DIM], k_all[..., DIFF_QK_DIM:], v_all,
                   q_pos, k_pos, rel_bias, lam)
    of = fox_core(fq, fk_all, fv_all, cum[:, P:], cum, q_pos, k_pos)
    return od, of


def gather_pages(cache, l, page_table):
    rows = cache[l, page_table]
    return rows.reshape(rows.shape[0], rows.shape[1] * rows.shape[2], *rows.shape[3:])


def mixer_sublayer(x, c, l, w_ada_l, b_ada_l, g_pre, g_post, w_in_l, b_f_l,
                   lq1, lk1, lq2, lk2, subln_l, w_out_l, attend):
    B, T, _ = x.shape
    mod = jnp.einsum('bd,de->be', jax.nn.silu(c), w_ada_l) + b_ada_l
    shift, scale, gate = jnp.split(mod[:, None, :], 3, axis=-1)
    h = rmsnorm(x, g_pre) * (1 + scale) + shift
    dq, dk, dv, dg, fq, fk, fv, fg, logf = project(h, w_in_l, b_f_l)
    lam_init = lambda_init(l)
    lam = (jnp.exp(jnp.sum(lq1.astype(jnp.float32) * lk1.astype(jnp.float32)))
           - jnp.exp(jnp.sum(lq2.astype(jnp.float32) * lk2.astype(jnp.float32))) + lam_init)
    od, of = attend(dq, dk, dv, fq, fk, fv, logf, lam)
    od = rmsnorm(od, subln_l) * (1 - lam_init)
    y = jnp.concatenate([od.reshape(B, T, DIFF_WIDTH) * jax.nn.silu(dg),
                         of.reshape(B, T, FOX_WIDTH) * jax.nn.silu(fg)], axis=-1)
    y = jnp.einsum('btm,md->btd', y, w_out_l)
    x = x + gate * rmsnorm(y, g_post)
    return x, (dk, dv, fk, fv, logf)


def setup_inputs(seed: int = 0) -> dict:
    key = jax.random.key(seed)
    ks = jax.random.split(key, 24)
    f32 = jnp.float32
    n_pages = PAST_LEN // PAGE_SIZE
    n_used = DEC_BATCH * n_pages
    n_pool = n_used + n_used // 4
    page_table = jax.random.permutation(ks[0], n_pool)[:n_used].reshape(DEC_BATCH, n_pages).astype(jnp.int32)
    nrm = lambda k, shape: jax.random.normal(k, shape, dtype=f32)
    return {
        "x_prompt": nrm(ks[1], (BATCH, SEQ, D_MODEL)),
        "x_sample": nrm(ks[2], (DEC_BATCH, DEC_SEQ, D_MODEL)),
        "c_prompt": nrm(ks[3], (BATCH, D_MODEL)),
        "c_sample": nrm(ks[4], (DEC_BATCH, D_MODEL)),
        "cache_diff_k": nrm(ks[5], (DEPTH, n_pool, PAGE_SIZE, N_DIFF_HEADS, DIFF_HEAD_DIM)),
        "cache_diff_v": nrm(ks[6], (DEPTH, n_pool, PAGE_SIZE, N_DIFF_HEADS, DIFF_HEAD_DIM)),
        "cache_fox_k": nrm(ks[7], (DEPTH, n_pool, PAGE_SIZE, N_FOX_HEADS, FOX_HEAD_DIM)),
        "cache_fox_v": nrm(ks[8], (DEPTH, n_pool, PAGE_SIZE, N_FOX_HEADS, FOX_HEAD_DIM)),
        "cache_fox_logf": jax.nn.log_sigmoid(3.0 + nrm(ks[9], (DEPTH, n_pool, PAGE_SIZE, N_FOX_HEADS))),
        "page_table": page_table,
        "w_ada": nrm(ks[10], (DEPTH, D_MODEL, 3 * D_MODEL)) * (0.5 * D_MODEL ** -0.5),
        "b_ada": 0.02 * nrm(ks[11], (DEPTH, 3 * D_MODEL)),
        "norm_pre": 1.0 + 0.02 * nrm(ks[12], (DEPTH, D_MODEL)),
        "norm_post": 1.0 + 0.02 * nrm(ks[13], (DEPTH, D_MODEL)),
        "w_in": nrm(ks[14], (DEPTH, D_MODEL, D_IN)) * D_MODEL ** -0.5,
        "b_forget": 3.0 + 0.5 * nrm(ks[15], (DEPTH, N_FOX_HEADS)),
        "lambda_q1": 0.1 * nrm(ks[16], (DEPTH, DIFF_QK_DIM)),
        "lambda_k1": 0.1 * nrm(ks[17], (DEPTH, DIFF_QK_DIM)),
        "lambda_q2": 0.1 * nrm(ks[18], (DEPTH, DIFF_QK_DIM)),
        "lambda_k2": 0.1 * nrm(ks[19], (DEPTH, DIFF_QK_DIM)),
        "diff_subln": 1.0 + 0.02 * nrm(ks[20], (DEPTH, DIFF_HEAD_DIM)),
        "rel_bias": 0.5 * nrm(ks[21], (N_BUCKETS, N_DIFF_HEADS)),
        "w_out": nrm(ks[22], (DEPTH, D_MIX, D_MODEL)) * D_MIX ** -0.5,
    }


def reference(x_prompt, x_sample, c_prompt, c_sample, cache_diff_k, cache_diff_v, cache_fox_k,
              cache_fox_v, cache_fox_logf, page_table, w_ada, b_ada, norm_pre, norm_post, w_in,
              b_forget, lambda_q1, lambda_k1, lambda_q2, lambda_k2, diff_subln, rel_bias, w_out):
    xp, xs = x_prompt, x_sample
    st_p = [[], [], [], [], []]
    st_s = [[], [], [], [], []]
    prompt_attend = functools.partial(prompt_mixers, rel_bias=rel_bias)
    for l in range(DEPTH):
        weights = (w_ada[l], b_ada[l], norm_pre[l], norm_post[l], w_in[l], b_forget[l],
                   lambda_q1[l], lambda_k1[l], lambda_q2[l], lambda_k2[l], diff_subln[l], w_out[l])
        xp, new_p = mixer_sublayer(xp, c_prompt, l, *weights, attend=prompt_attend)
        sample_attend = functools.partial(
            sample_mixers,
            past_dk=gather_pages(cache_diff_k, l, page_table),
            past_dv=gather_pages(cache_diff_v, l, page_table),
            past_fk=gather_pages(cache_fox_k, l, page_table),
            past_fv=gather_pages(cache_fox_v, l, page_table),
            past_logf=gather_pages(cache_fox_logf, l, page_table),
            rel_bias=rel_bias)
        xs, new_s = mixer_sublayer(xs, c_sample, l, *weights, attend=sample_attend)
        for i in range(5):
            st_p[i].append(new_p[i])
            st_s[i].append(new_s[i])
    k_diff_p, v_diff_p, k_fox_p, v_fox_p, logf_p = [jnp.stack(a, axis=0) for a in st_p]
    k_diff_s, v_diff_s, k_fox_s, v_fox_s, logf_s = [jnp.stack(a, axis=0) for a in st_s]
    return (xp, xs, k_diff_p, v_diff_p, k_fox_p, v_fox_p, logf_p,
            k_diff_s, v_diff_s, k_fox_s, v_fox_s, logf_s)
```

```python
import functools
import math

import numpy as np
import jax
import jax.numpy as jnp
from jax import lax
from jax.experimental import pallas as pl
from jax.experimental.pallas import tpu as pltpu

F32 = jnp.float32
BF16 = jnp.bfloat16

EPS = 1e-6
HEAD_DIM = 128
DIFF_QK_DIM = 64
N_HEADS = 8
N_BUCKETS = 32
MAX_DISTANCE = 128
NEG_BIG = -1e30

V7X_VMEM_LIMIT_BYTES = 56 * 1024 * 1024
LANE = 128
SUBLANE = 8

ATTN_BLOCK = 256
DEC_CHUNK_PAGES = 4
DEC_SLOTS = 3


def _cparams(*sem):
    return pltpu.CompilerParams(dimension_semantics=sem, vmem_limit_bytes=V7X_VMEM_LIMIT_BYTES)


def _lambda_init(l):
    return 0.8 - 0.6 * math.exp(-0.3 * l)


def _log_sigmoid(x):
    return jnp.minimum(x, 0.0) - jnp.log1p(jnp.exp(-jnp.abs(x)))


def _silu(x):
    return x * jax.nn.sigmoid(x)


def _dot(a, b):
    return jnp.dot(a, b, preferred_element_type=F32)


def _dot_nt(a, b):
    return lax.dot_general(a, b, (((1,), (1,)), ((), ())), preferred_element_type=F32)


def _ada_kernel(c_ref, w_ref, b_ref, o_ref):
    sc = _silu(c_ref[...]).astype(BF16)
    o_ref[...] = _dot(sc, w_ref[...].astype(BF16)) + b_ref[...]


def _ada_call(c_all, w_ada, b_ada, tn=768):
    depth, d, e = w_ada.shape
    rows = c_all.shape[0]
    return pl.pallas_call(
        _ada_kernel,
        grid=(depth, e // tn),
        in_specs=[
            pl.BlockSpec((rows, d), lambda l, j: (0, 0)),
            pl.BlockSpec((None, d, tn), lambda l, j: (l, 0, j)),
            pl.BlockSpec((None, 1, tn), lambda l, j: (l, 0, j)),
        ],
        out_specs=pl.BlockSpec((None, rows, tn), lambda l, j: (l, 0, j)),
        out_shape=jax.ShapeDtypeStruct((depth, rows, e), F32),
        compiler_params=_cparams("arbitrary", "arbitrary"),
        name="ada_mod",
    )(c_all, w_ada, b_ada.reshape(depth, 1, e))


def _bias_kernel(rel_ref, idx_ref, o_ref):
    h = pl.program_id(0)
    idx = idx_ref[...]
    acc = jnp.zeros(idx.shape, F32)
    for b in range(N_BUCKETS):
        acc = jnp.where(idx == b, rel_ref[b, h], acc)
    o_ref[...] = acc


def _bias_call(rel_bias, idx):
    r, c = idx.shape
    return pl.pallas_call(
        _bias_kernel,
        grid=(N_HEADS,),
        in_specs=[
            pl.BlockSpec(memory_space=pltpu.SMEM),
            pl.BlockSpec((r, c), lambda h: (0, 0)),
        ],
        out_specs=pl.BlockSpec((None, r, c), lambda h: (h, 0, 0)),
        out_shape=jax.ShapeDtypeStruct((N_HEADS, r, c), F32),
        compiler_params=_cparams("arbitrary"),
        name="rel_bias_tables",
    )(rel_bias, idx)


def _t5_bucket(n):
    max_exact = N_BUCKETS // 2
    large = max_exact + (jnp.log(jnp.maximum(n, 1).astype(F32) / max_exact)
                         / math.log(MAX_DISTANCE / max_exact) * (N_BUCKETS - max_exact)).astype(jnp.int32)
    large = jnp.minimum(large, N_BUCKETS - 1)
    return jnp.where(n < max_exact, n, large)


def _far_distance():
    n = np.arange(1, 4 * MAX_DISTANCE)
    large = 16 + (np.log(n.astype(np.float32) / 16) / math.log(MAX_DISTANCE / 16) * 16).astype(np.int32)
    bucket = np.where(n < 16, n, np.minimum(large, N_BUCKETS - 1))
    not_last = np.nonzero(bucket != N_BUCKETS - 1)[0]
    return int(n[not_last[-1]]) + 1


def _proj_kernel(x_ref, shift_ref, scale_ref, g_ref, w_ref, wf_ref, wft_ref, bfr_ref, bfc_ref,
                 z_ref, lf_ref, lft_ref, h_scr):
    @pl.when(pl.program_id(1) == 0)
    def _():
        x = x_ref[...]
        ms = jnp.mean(x * x, axis=-1, keepdims=True)
        y = x * lax.rsqrt(ms + EPS) * g_ref[...]
        h = (y * (1.0 + scale_ref[...]) + shift_ref[...]).astype(BF16)
        h_scr[...] = h
        ff = _dot(h, wf_ref[...])[:, :N_HEADS] + bfr_ref[...]
        lf_ref[...] = _log_sigmoid(ff)
        fft = _dot_nt(wft_ref[...], h)[:N_HEADS, :] + bfc_ref[...]
        lft_ref[...] = _log_sigmoid(fft)

    z_ref[...] = _dot(h_scr[...], w_ref[...])


def _proj_call(x, shift, scale, g_pre, w_in_bf, wf_pad, wft_pad, bf_row, bf_col, l, tm, name):
    m, d = x.shape
    groups, r, _ = shift.shape
    tiles_per_group = (m // tm) // groups
    width = (w_in_bf.shape[2] - N_HEADS) // 8
    mod_spec = pl.BlockSpec((None, r, d), lambda i, j: (i // tiles_per_group, 0, 0))
    return pl.pallas_call(
        _proj_kernel,
        grid=(m // tm, 8),
        in_specs=[
            pl.BlockSpec((tm, d), lambda i, j: (i, 0)),
            mod_spec, mod_spec,
            pl.BlockSpec((None, 1, d), lambda i, j: (l, 0, 0)),
            pl.BlockSpec((None, d, width), lambda i, j: (l, 0, j)),
            pl.BlockSpec((None, d, LANE), lambda i, j: (l, 0, 0)),
            pl.BlockSpec((None, 2 * SUBLANE, d), lambda i, j: (l, 0, 0)),
            pl.BlockSpec((None, 1, N_HEADS), lambda i, j: (l, 0, 0)),
            pl.BlockSpec((None, N_HEADS, 1), lambda i, j: (l, 0, 0)),
        ],
        out_specs=[
            pl.BlockSpec((None, tm, width), lambda i, j: (j, i, 0)),
            pl.BlockSpec((tm, N_HEADS), lambda i, j: (i, 0)),
            pl.BlockSpec((N_HEADS, tm), lambda i, j: (0, i)),
        ],
        out_shape=[
            jax.ShapeDtypeStruct((8, m, width), F32),
            jax.ShapeDtypeStruct((m, N_HEADS), F32),
            jax.ShapeDtypeStruct((N_HEADS, m), F32),
        ],
        scratch_shapes=[pltpu.VMEM((tm, d), BF16)],
        compiler_params=_cparams("arbitrary", "arbitrary"),
        name=name,
    )(x, shift, scale, g_pre, w_in_bf, wf_pad, wft_pad, bf_row, bf_col)


def _cum_kernel(x_ref, row_ref, col_ref):
    x = x_ref[...]
    t = x.shape[1]
    lane = lax.broadcasted_iota(jnp.int32, x.shape, 1)
    s = 1
    while s < t:
        x = x + jnp.where(lane >= s, pltpu.roll(x, s, 1), 0.0)
        s *= 2
    row_ref[...] = x
    xp = jnp.concatenate([x, jnp.zeros((LANE - x.shape[0], t), F32)], axis=0)
    col_ref[...] = xp.T


def _cum_call(logft, batch):
    nh, m = logft.shape
    t = m // batch
    return pl.pallas_call(
        _cum_kernel,
        grid=(batch,),
        in_specs=[pl.BlockSpec((nh, t), lambda b: (0, b))],
        out_specs=[
            pl.BlockSpec((None, nh, t), lambda b: (b, 0, 0)),
            pl.BlockSpec((None, t, LANE), lambda b: (b, 0, 0)),
        ],
        out_shape=[
            jax.ShapeDtypeStruct((batch, nh, t), F32),
            jax.ShapeDtypeStruct((batch, t, LANE), F32),
        ],
        compiler_params=_cparams("arbitrary"),
        name="prompt_logf_cumsum",
    )(logft)


def _lambda_full(lq1_ref, lk1_ref, lq2_ref, lk2_ref, lam_init):
    a = jnp.sum(lq1_ref[...] * lk1_ref[...], axis=-1, keepdims=True)
    b = jnp.sum(lq2_ref[...] * lk2_ref[...], axis=-1, keepdims=True)
    return jnp.exp(a) - jnp.exp(b) + lam_init


def _diff_rows(q, scale):
    lane = lax.broadcasted_iota(jnp.int32, q.shape, 1)
    qs = q * scale
    return jnp.concatenate([jnp.where(lane < DIFF_QK_DIM, qs, 0.0),
                            jnp.where(lane >= DIFF_QK_DIM, qs, 0.0)], axis=0).astype(BF16)


def _online_update(state, logits, shift_back, v):
    m, l, acc = state
    m_new = jnp.maximum(m, jnp.max(logits, axis=-1, keepdims=True) + shift_back)
    p = jnp.exp(logits - (m_new - shift_back))
    alpha = jnp.exp(m - m_new)
    l = alpha * l + jnp.sum(p, axis=-1, keepdims=True)
    acc = alpha * acc + _dot(p.astype(BF16), v)
    return m_new, l, acc


def _diff_finish(acc, l, rows, lam, lam_init, subln, gate):
    o = acc[:rows] / l[:rows] - lam * (acc[rows:] / l[rows:])
    o = o * lax.rsqrt(jnp.mean(o * o, axis=-1, keepdims=True) + EPS) * subln * (1.0 - lam_init)
    return o * _silu(gate)


def _attn_diff_kernel(rel_ref, q_ref, k_ref, v_ref, g_ref, bias_ref, lq1_ref, lk1_ref, lq2_ref, lk2_ref,
                      subln_ref, o_ref, k_scr, v_scr, *, lam_init, blk):
    t = q_ref.shape[0]
    nq = t // blk
    scale = DIFF_QK_DIM ** -0.5
    c_far = rel_ref[N_BUCKETS - 1, pl.program_id(1)]
    k_scr[...] = k_ref[...].astype(BF16)
    v_scr[...] = v_ref[...].astype(BF16)
    lam = _lambda_full(lq1_ref, lk1_ref, lq2_ref, lk2_ref, lam_init)
    row = lax.broadcasted_iota(jnp.int32, (blk, blk), 0)
    col = lax.broadcasted_iota(jnp.int32, (blk, blk), 1)
    causal = jnp.concatenate([col <= row, col <= row], axis=0)

    for qi in range(nq):
        qq = _diff_rows(q_ref[qi * blk:(qi + 1) * blk, :], scale)
        state = (jnp.full((2 * blk, 1), NEG_BIG, F32), jnp.zeros((2 * blk, 1), F32),
                 jnp.zeros((2 * blk, HEAD_DIM), F32))

        def far_body(kj, st, qq=qq):
            ks = pl.multiple_of(kj * blk, blk)
            s = _dot_nt(qq, k_scr[pl.ds(ks, blk), :])
            return _online_update(st, s, c_far, v_scr[pl.ds(ks, blk), :])

        if qi >= 2:
            state = lax.fori_loop(0, qi - 1, far_body, state)
        if qi >= 1:
            ks = (qi - 1) * blk
            b1 = bias_ref[1]
            s = _dot_nt(qq, k_scr[ks:ks + blk, :]) + jnp.concatenate([b1, b1], axis=0)
            state = _online_update(state, s, 0.0, v_scr[ks:ks + blk, :])
        ks = qi * blk
        b0 = bias_ref[0]
        s = _dot_nt(qq, k_scr[ks:ks + blk, :]) + jnp.concatenate([b0, b0], axis=0)
        s = jnp.where(causal, s, NEG_BIG)
        _, l, acc = _online_update(state, s, 0.0, v_scr[ks:ks + blk, :])
        y = _diff_finish(acc, l, blk, lam, lam_init, subln_ref[...], g_ref[qi * blk:(qi + 1) * blk, :])
        o_ref[qi * blk:(qi + 1) * blk, :] = y.astype(BF16)


def _attn_diff_call(z, bias_p, rel_bias, lq1, lk1, lq2, lk2, subln, l, batch, blk):
    _, m, width = z.shape
    t = m // batch
    nh = width // HEAD_DIM

    def zspec(k):
        return pl.BlockSpec((None, t, HEAD_DIM), lambda b, h: (k, b, h))

    def pspec(n):
        return pl.BlockSpec((None, 1, n), lambda b, h: (l, 0, 0))

    return pl.pallas_call(
        functools.partial(_attn_diff_kernel, lam_init=_lambda_init(l), blk=blk),
        grid=(batch, nh),
        in_specs=[
            pl.BlockSpec(memory_space=pltpu.SMEM),
            zspec(0), zspec(1), zspec(2), zspec(3),
            pl.BlockSpec((None, 2, blk, blk), lambda b, h: (h, 0, 0, 0)),
            pspec(DIFF_QK_DIM), pspec(DIFF_QK_DIM), pspec(DIFF_QK_DIM), pspec(DIFF_QK_DIM),
            pspec(HEAD_DIM),
        ],
        out_specs=pl.BlockSpec((t, HEAD_DIM), lambda b, h: (b, h)),
        out_shape=jax.ShapeDtypeStruct((m, width), BF16),
        scratch_shapes=[pltpu.VMEM((t, HEAD_DIM), BF16), pltpu.VMEM((t, HEAD_DIM), BF16)],
        compiler_params=_cparams("arbitrary", "arbitrary"),
        name="prompt_diff_attention",
    )(rel_bias, z, z, z, z, bias_p, lq1, lk1, lq2, lk2, subln)


def _attn_fox_kernel(q_ref, k_ref, v_ref, g_ref, ck_ref, cc_ref, o_ref, k_scr, v_scr, *, blk):
    t = q_ref.shape[0]
    nq = t // blk
    scale = HEAD_DIM ** -0.5
    h = pl.program_id(1)
    k_scr[...] = k_ref[...].astype(BF16)
    v_scr[...] = v_ref[...].astype(BF16)
    row = lax.broadcasted_iota(jnp.int32, (blk, blk), 0)
    col = lax.broadcasted_iota(jnp.int32, (blk, blk), 1)
    causal = col <= row
    head_lane = lax.broadcasted_iota(jnp.int32, (blk, LANE), 1) == h

    for qi in range(nq):
        qs = slice(qi * blk, (qi + 1) * blk)
        qq = (q_ref[qs, :] * scale).astype(BF16)
        cq = jnp.sum(jnp.where(head_lane, cc_ref[qs, :], 0.0), axis=-1, keepdims=True)
        state = (jnp.full((blk, 1), NEG_BIG, F32), jnp.zeros((blk, 1), F32), jnp.zeros((blk, HEAD_DIM), F32))

        def body(kj, st, qq=qq, cq=cq):
            ks = pl.multiple_of(kj * blk, blk)
            s = _dot_nt(qq, k_scr[pl.ds(ks, blk), :]) - ck_ref[kj]
            return _online_update(st, s, cq, v_scr[pl.ds(ks, blk), :])

        if qi >= 1:
            state = lax.fori_loop(0, qi, body, state)
        ks = qi * blk
        s = _dot_nt(qq, k_scr[ks:ks + blk, :]) - ck_ref[qi]
        s = jnp.where(causal, s, NEG_BIG)
        _, l, acc = _online_update(state, s, cq, v_scr[ks:ks + blk, :])
        o_ref[qs, :] = (acc / l * _silu(g_ref[qs, :])).astype(BF16)


def _attn_fox_call(z, ck_rows, cum_cols, batch, blk):
    _, m, width = z.shape
    t = m // batch
    nh = width // HEAD_DIM

    def zspec(k):
        return pl.BlockSpec((None, t, HEAD_DIM), lambda b, h: (k, b, h))

    return pl.pallas_call(
        functools.partial(_attn_fox_kernel, blk=blk),
        grid=(batch, nh),
        in_specs=[
            zspec(4), zspec(5), zspec(6), zspec(7),
            pl.BlockSpec((None, t // blk, 1, blk), lambda b, h: (b * nh + h, 0, 0, 0)),
            pl.BlockSpec((None, t, LANE), lambda b, h: (b, 0, 0)),
        ],
        out_specs=pl.BlockSpec((t, HEAD_DIM), lambda b, h: (b, h)),
        out_shape=jax.ShapeDtypeStruct((m, width), BF16),
        scratch_shapes=[pltpu.VMEM((t, HEAD_DIM), BF16), pltpu.VMEM((t, HEAD_DIM), BF16)],
        compiler_params=_cparams("arbitrary", "arbitrary"),
        name="prompt_fox_attention",
    )(z, z, z, z, ck_rows.reshape(batch * nh, t // blk, 1, blk), cum_cols)


def _out_kernel(yd_ref, yf_ref, w_ref, x_ref, gate_ref, g_ref, o_ref):
    half = yd_ref.shape[1]
    y = _dot(yd_ref[...], w_ref[:half, :]) + _dot(yf_ref[...], w_ref[half:, :])
    r = y * lax.rsqrt(jnp.mean(y * y, axis=-1, keepdims=True) + EPS) * g_ref[...]
    o_ref[...] = x_ref[...] + gate_ref[...] * r


def _out_call(yd, yf, w_out_bf, x, gate, g_post, l, tm, name):
    m, d = x.shape
    half = yd.shape[1]
    groups, r, _ = gate.shape
    tiles_per_group = (m // tm) // groups
    return pl.pallas_call(
        _out_kernel,
        grid=(m // tm,),
        in_specs=[
            pl.BlockSpec((tm, half), lambda i: (i, 0)),
            pl.BlockSpec((tm, half), lambda i: (i, 0)),
            pl.BlockSpec((None, 2 * half, d), lambda i: (l, 0, 0)),
            pl.BlockSpec((tm, d), lambda i: (i, 0)),
            pl.BlockSpec((None, r, d), lambda i: (i // tiles_per_group, 0, 0)),
            pl.BlockSpec((None, 1, d), lambda i: (l, 0, 0)),
        ],
        out_specs=pl.BlockSpec((tm, d), lambda i: (i, 0)),
        out_shape=jax.ShapeDtypeStruct((m, d), F32),
        compiler_params=_cparams("arbitrary"),
        name=name,
    )(yd, yf, w_out_bf, x, gate, g_post)


def _dec_kernel(pt_ref, rel_ref, zs_ref, lfn_ref, bl_ref, bn_ref, lq1_ref, lk1_ref, lq2_ref, lk2_ref, subln_ref,
                cdk_ref, cdv_ref, cfk_ref, cfv_ref, clf_ref,
                yd_ref, yf_ref,
                kv_buf, lf_buf, sem, new_kv, m_scr, l_scr, acc_scr, carry_scr,
                *, layer, lam_init, n_chunks, n_batch):
    b = pl.program_id(0)
    cp = DEC_CHUNK_PAGES
    page = lf_buf.shape[2]
    tc = cp * page
    t_new = zs_ref.shape[1]
    total = n_batch * n_chunks
    caches = (cdk_ref, cdv_ref, cfk_ref, cfv_ref)

    def chunk_copies(g, slot):
        gb = g // n_chunks
        gc = g % n_chunks
        out = []
        for j in range(cp):
            pg = pt_ref[gb, gc * cp + j]
            for ti, c_ref in enumerate(caches):
                out.append(pltpu.make_async_copy(c_ref.at[layer, pg], kv_buf.at[slot, ti, pl.ds(j * page, page), :],
                                                 sem.at[slot]))
            out.append(pltpu.make_async_copy(clf_ref.at[layer, pg], lf_buf.at[slot, j], sem.at[slot]))
        return out

    @pl.when(b == 0)
    def _():
        for g in range(DEC_SLOTS - 1):
            for c in chunk_copies(g, g):
                c.start()

    m_scr[...] = jnp.full(m_scr.shape, NEG_BIG, F32)
    l_scr[...] = jnp.zeros(l_scr.shape, F32)
    acc_scr[...] = jnp.zeros(acc_scr.shape, F32)
    carry_scr[...] = jnp.zeros(carry_scr.shape, F32)

    scale_d = DIFF_QK_DIM ** -0.5
    scale_f = HEAD_DIM ** -0.5
    zq_d = zs_ref[0]
    zq_f = zs_ref[4]
    pad_rows = jnp.zeros((SUBLANE - t_new, HEAD_DIM), F32)
    qd = [_diff_rows(zq_d[:, h * HEAD_DIM:(h + 1) * HEAD_DIM], scale_d) for h in range(N_HEADS)]
    qf = [jnp.concatenate([zq_f[:, h * HEAD_DIM:(h + 1) * HEAD_DIM] * scale_f, pad_rows], axis=0).astype(BF16)
          for h in range(N_HEADS)]
    c_far = [rel_ref[N_BUCKETS - 1, h] for h in range(N_HEADS)]

    tri_r = lax.broadcasted_iota(jnp.int32, (page, page), 0)
    tri_c = lax.broadcasted_iota(jnp.int32, (page, page), 1)
    upper = (tri_r <= tri_c).astype(BF16)
    lane_tc = lax.broadcasted_iota(jnp.int32, (SUBLANE, tc), 1)

    def head_update(idx, logits, shift_back, v):
        st = (m_scr[idx], l_scr[idx], acc_scr[idx])
        m_new, l_new, acc_new = _online_update(st, logits, shift_back, v)
        m_scr[idx] = m_new
        l_scr[idx] = l_new
        acc_scr[idx] = acc_new

    def chunk_body(c, _):
        g = b * n_chunks + c
        slot = g % DEC_SLOTS
        nxt = g + DEC_SLOTS - 1

        @pl.when(nxt < total)
        def _():
            for cpy in chunk_copies(nxt, nxt % DEC_SLOTS):
                cpy.start()

        for cpy in chunk_copies(g, slot):
            cpy.wait()

        ck_parts = []
        carry = carry_scr[...]
        for j in range(cp):
            xt = lf_buf[slot, j].T
            hi = xt.astype(BF16)
            r1 = xt - hi.astype(F32)
            mid = r1.astype(BF16)
            lo = (r1 - mid.astype(F32)).astype(BF16)
            cs = _dot(hi, upper) + _dot(mid, upper) + _dot(lo, upper) + carry
            ck_parts.append(cs)
            carry = cs[:, page - 1:page]
        carry_scr[...] = carry
        ck = jnp.concatenate(ck_parts, axis=1)

        is_last = c == n_chunks - 1
        for h in range(N_HEADS):
            hs = slice(h * HEAD_DIM, (h + 1) * HEAD_DIM)
            kh = kv_buf[slot, 0, :, hs].astype(BF16)
            vh = kv_buf[slot, 1, :, hs].astype(BF16)
            s = _dot_nt(qd[h], kh)
            near = jnp.concatenate([jnp.full((SUBLANE, tc - page), c_far[h], F32), bl_ref[h]], axis=1)
            s = s + jnp.where(is_last, near, c_far[h])
            head_update(h, s, 0.0, vh)
            kh = kv_buf[slot, 2, :, hs].astype(BF16)
            vh = kv_buf[slot, 3, :, hs].astype(BF16)
            s = _dot_nt(qf[h], kh) - ck[h:h + 1, :]
            head_update(N_HEADS + h, s, 0.0, vh)
        return 0

    lax.fori_loop(0, n_chunks, chunk_body, 0)

    new_kv[...] = jnp.zeros(new_kv.shape, F32)
    for ti, zi in enumerate((1, 2, 5, 6)):
        new_kv[ti, 0:t_new, :] = zs_ref[zi]
    lam = _lambda_full(lq1_ref, lk1_ref, lq2_ref, lk2_ref, lam_init)
    rowq = lax.broadcasted_iota(jnp.int32, (SUBLANE, LANE), 0) % t_new
    colk = lax.broadcasted_iota(jnp.int32, (SUBLANE, LANE), 1)
    causal_new = colk <= rowq
    ck_past_end = carry_scr[...]
    lfn = lfn_ref[...]
    cnew = [lfn[:, 0:1] + ck_past_end]
    for j in range(1, t_new):
        cnew.append(cnew[-1] + lfn[:, j:j + 1])
    ck_new = jnp.zeros((SUBLANE, LANE), F32)
    for j in range(t_new):
        ck_new = jnp.where(colk == j, cnew[j], ck_new)

    for h in range(N_HEADS):
        hs = slice(h * HEAD_DIM, (h + 1) * HEAD_DIM)
        s = _dot_nt(qd[h], new_kv[0, :, hs].astype(BF16)) + bn_ref[h]
        s = jnp.where(causal_new, s, NEG_BIG)
        head_update(h, s, 0.0, new_kv[1, :, hs].astype(BF16))
        y = _diff_finish(acc_scr[h], l_scr[h], t_new, lam, lam_init, subln_ref[...], zs_ref[3][:, hs])
        yd_ref[:, hs] = y.astype(BF16)

        cq = jnp.zeros((SUBLANE, 1), F32)
        for j in range(t_new):
            cq = jnp.where(rowq[:, 0:1] == j, cnew[j][h:h + 1, :], cq)
        s = _dot_nt(qf[h], new_kv[2, :, hs].astype(BF16)) - ck_new[h:h + 1, :]
        s = jnp.where(causal_new, s, NEG_BIG)
        idx = N_HEADS + h
        m_all = m_scr[idx] + cq
        m_new = jnp.maximum(m_all, jnp.max(s, axis=-1, keepdims=True) + cq)
        p = jnp.exp(s - (m_new - cq))
        alpha = jnp.exp(m_all - m_new)
        l_fin = alpha * l_scr[idx] + jnp.sum(p, axis=-1, keepdims=True)
        acc_fin = alpha * acc_scr[idx] + _dot(p.astype(BF16), new_kv[3, :, hs].astype(BF16))
        yf = acc_fin[:t_new] / l_fin[:t_new] * _silu(zs_ref[7][:, hs])
        yf_ref[:, hs] = yf.astype(BF16)


def _dec_call(page_table, rel_bias, zs, lfn, bias_last, bias_new, lq1, lk1, lq2, lk2, subln,
              cdk, cdv, cfk, cfv, clf, l):
    _, nb, t_new, width = zs.shape
    n_pages = page_table.shape[1]
    page = cdk.shape[2]
    n_chunks = n_pages // DEC_CHUNK_PAGES
    tc = DEC_CHUNK_PAGES * page

    def pspec(n):
        return pl.BlockSpec((None, 1, n), lambda b, pt: (l, 0, 0))

    any_spec = pl.BlockSpec(memory_space=pl.ANY)
    grid_spec = pltpu.PrefetchScalarGridSpec(
        num_scalar_prefetch=1,
        grid=(nb,),
        in_specs=[
            pl.BlockSpec(memory_space=pltpu.SMEM),
            pl.BlockSpec((8, None, t_new, width), lambda b, pt: (0, b, 0, 0)),
            pl.BlockSpec((None, N_HEADS, t_new), lambda b, pt: (b, 0, 0)),
            pl.BlockSpec((N_HEADS, SUBLANE, LANE), lambda b, pt: (0, 0, 0)),
            pl.BlockSpec((N_HEADS, SUBLANE, LANE), lambda b, pt: (0, 0, 0)),
            pspec(DIFF_QK_DIM), pspec(DIFF_QK_DIM), pspec(DIFF_QK_DIM), pspec(DIFF_QK_DIM),
            pspec(HEAD_DIM),
            any_spec, any_spec, any_spec, any_spec, any_spec,
        ],
        out_specs=[
            pl.BlockSpec((None, t_new, width), lambda b, pt: (b, 0, 0)),
            pl.BlockSpec((None, t_new, width), lambda b, pt: (b, 0, 0)),
        ],
        scratch_shapes=[
            pltpu.VMEM((DEC_SLOTS, 4, tc, width), F32),
            pltpu.VMEM((DEC_SLOTS, DEC_CHUNK_PAGES, page, N_HEADS), F32),
            pltpu.SemaphoreType.DMA((DEC_SLOTS,)),
            pltpu.VMEM((4, LANE, width), F32),
            pltpu.VMEM((2 * N_HEADS, SUBLANE, 1), F32),
            pltpu.VMEM((2 * N_HEADS, SUBLANE, 1), F32),
            pltpu.VMEM((2 * N_HEADS, SUBLANE, HEAD_DIM), F32),
            pltpu.VMEM((N_HEADS, 1), F32),
        ],
    )
    return pl.pallas_call(
        functools.partial(_dec_kernel, layer=l, lam_init=_lambda_init(l), n_chunks=n_chunks, n_batch=nb),
        grid_spec=grid_spec,
        out_shape=[
            jax.ShapeDtypeStruct((nb, t_new, width), BF16),
            jax.ShapeDtypeStruct((nb, t_new, width), BF16),
        ],
        compiler_params=_cparams("arbitrary"),
        name="sample_paged_attention",
    )(page_table, rel_bias, zs, lfn, bias_last, bias_new, lq1, lk1, lq2, lk2, subln, cdk, cdv, cfk, cfv, clf)


def kernel(x_prompt, x_sample, c_prompt, c_sample, cache_diff_k, cache_diff_v, cache_fox_k, cache_fox_v, cache_fox_logf, page_table, w_ada, b_ada, norm_pre, norm_post, w_in, b_forget, lambda_q1, lambda_k1, lambda_q2, lambda_k2, diff_subln, rel_bias, w_out):
    batch, seq, d = x_prompt.shape
    nb, t_new, _ = x_sample.shape
    depth = w_ada.shape[0]
    n_pool, page = cache_diff_k.shape[1], cache_diff_k.shape[2]
    past = page_table.shape[1] * page
    width = N_HEADS * HEAD_DIM
    blk = ATTN_BLOCK
    m_p, m_s = batch * seq, nb * t_new

    far = _far_distance()
    assert far <= blk + 1 and far <= page + 1

    w_in_bf = w_in.astype(BF16)
    wf = w_in[:, :, 8 * width:]
    wf_pad = jnp.pad(wf, ((0, 0), (0, 0), (0, LANE - N_HEADS))).astype(BF16)
    wft_pad = jnp.pad(jnp.swapaxes(wf, 1, 2), ((0, 0), (0, 2 * SUBLANE - N_HEADS), (0, 0))).astype(BF16)
    w_out_bf = w_out.astype(BF16)
    bf_row = b_forget.reshape(depth, 1, N_HEADS)
    bf_col = b_forget.reshape(depth, N_HEADS, 1)
    g_pre = norm_pre.reshape(depth, 1, d)
    g_post = norm_post.reshape(depth, 1, d)
    lq1 = lambda_q1.reshape(depth, 1, DIFF_QK_DIM)
    lk1 = lambda_k1.reshape(depth, 1, DIFF_QK_DIM)
    lq2 = lambda_q2.reshape(depth, 1, DIFF_QK_DIM)
    lk2 = lambda_k2.reshape(depth, 1, DIFF_QK_DIM)
    subln = diff_subln.reshape(depth, 1, HEAD_DIM)

    rows = -(-(batch + nb) // (2 * SUBLANE)) * (2 * SUBLANE)
    c_all = jnp.concatenate([c_prompt, c_sample, jnp.zeros((rows - batch - nb, d), F32)], axis=0)
    mod = _ada_call(c_all, w_ada, b_ada)

    r_i = jnp.arange(blk, dtype=jnp.int32)
    dist = r_i[:, None] - r_i[None, :]
    idx_p = jnp.concatenate([_t5_bucket(jnp.maximum(dist, 0)), _t5_bucket(dist + blk)], axis=0)
    tq = jnp.arange(SUBLANE, dtype=jnp.int32)[:, None] % t_new
    cj = jnp.arange(LANE, dtype=jnp.int32)[None, :]
    idx_last = _t5_bucket(tq + page - cj)
    idx_new = _t5_bucket(jnp.maximum(tq - cj, 0))
    bias_p = _bias_call(rel_bias, idx_p).reshape(N_HEADS, 2, blk, blk)
    bias_dec = _bias_call(rel_bias, jnp.concatenate([idx_last, idx_new], axis=1))
    bias_last, bias_new = bias_dec[:, :, :LANE], bias_dec[:, :, LANE:]

    cdk = cache_diff_k.reshape(depth, n_pool, page, width)
    cdv = cache_diff_v.reshape(depth, n_pool, page, width)
    cfk = cache_fox_k.reshape(depth, n_pool, page, width)
    cfv = cache_fox_v.reshape(depth, n_pool, page, width)

    xp = x_prompt.reshape(m_p, d)
    xs = x_sample.reshape(m_s, d)
    tm_p = min(1024, seq)
    outs_p = [[] for _ in range(5)]
    outs_s = [[] for _ in range(5)]
    for l in range(depth):
        mp = mod[l, :batch]
        ms = jnp.repeat(mod[l, batch:batch + nb], t_new, axis=0)
        shift_p, scale_p, gate_p = (mp[:, None, k * d:(k + 1) * d] for k in range(3))
        shift_s, scale_s, gate_s = (ms[None, :, k * d:(k + 1) * d] for k in range(3))

        z, logf, logft = _proj_call(xp, shift_p, scale_p, g_pre, w_in_bf, wf_pad, wft_pad, bf_row, bf_col,
                                    l, tm_p, "prompt_in_proj")
        ck_rows, cum_cols = _cum_call(logft, batch)
        yd = _attn_diff_call(z, bias_p, rel_bias, lq1, lk1, lq2, lk2, subln, l, batch, blk)
        yf = _attn_fox_call(z, ck_rows, cum_cols, batch, blk)
        xp = _out_call(yd, yf, w_out_bf, xp, gate_p, g_post, l, min(512, seq), "prompt_out_proj")
        for i, a in enumerate((z[1], z[2], z[5], z[6])):
            outs_p[i].append(a.reshape(batch, seq, N_HEADS, HEAD_DIM))
        outs_p[4].append(logf.reshape(batch, seq, N_HEADS))

        zs, logf_s, logft_s = _proj_call(xs, shift_s, scale_s, g_pre, w_in_bf, wf_pad, wft_pad, bf_row, bf_col,
                                         l, m_s, "sample_in_proj")
        lfn = jnp.transpose(logft_s.reshape(N_HEADS, nb, t_new), (1, 0, 2))
        yds, yfs = _dec_call(page_table, rel_bias, zs.reshape(8, nb, t_new, width), lfn, bias_last, bias_new,
                             lq1, lk1, lq2, lk2, subln, cdk, cdv, cfk, cfv, cache_fox_logf, l)
        xs = _out_call(yds.reshape(m_s, width), yfs.reshape(m_s, width), w_out_bf, xs, gate_s, g_post,
                       l, m_s, "sample_out_proj")
        for i, a in enumerate((zs[1], zs[2], zs[5], zs[6])):
            outs_s[i].append(a.reshape(nb, t_new, N_HEADS, HEAD_DIM))
        outs_s[4].append(logf_s.reshape(nb, t_new, N_HEADS))

    stacked_p = [jnp.stack(a, axis=0) for a in outs_p]
    stacked_s = [jnp.stack(a, axis=0) for a in outs_s]
    return (xp.reshape(batch, seq, d), xs.reshape(nb, t_new, d), *stacked_p, *stacked_s)
```

```python
import functools
import math

import numpy as np
import jax
import jax.numpy as jnp
from jax import lax
from jax.experimental import pallas as pl
from jax.experimental.pallas import tpu as pltpu

F32 = jnp.float32
BF16 = jnp.bfloat16

EPS = 1e-6
HEAD_DIM = 128
DIFF_QK_DIM = 64
N_HEADS = 8
N_BUCKETS = 32
MAX_DISTANCE = 128
NEG_BIG = -1e30

V7X_VMEM_LIMIT_BYTES = 56 * 1024 * 1024
LANE = 128
SUBLANE = 8
MXU_COLS = 256

ATTN_BLOCK = 256
DEC_CHUNK_PAGES = 4
DEC_SLOTS = 3

G_DQ, G_DK, G_DV, G_DG, G_FQ, G_FK, G_FV, G_FG = range(8)
KV_GROUPS = (G_DK, G_DV, G_FK, G_FV)


def _cparams(*sem):
    return pltpu.CompilerParams(dimension_semantics=sem, vmem_limit_bytes=V7X_VMEM_LIMIT_BYTES)


def _lambda_init(l):
    return 0.8 - 0.6 * math.exp(-0.3 * l)


def _log_sigmoid(x):
    return jnp.minimum(x, 0.0) - jnp.log1p(jnp.exp(-jnp.abs(x)))


def _silu(x):
    return x * jax.nn.sigmoid(x)


def _dot(a, b):
    return jnp.dot(a, b, preferred_element_type=F32)


def _dot_nt(a, b):
    return lax.dot_general(a, b, (((1,), (1,)), ((), ())), preferred_element_type=F32)


def _ada_kernel(c_ref, w_ref, b_ref, o_ref):
    sc = _silu(c_ref[...]).astype(BF16)
    o_ref[...] = _dot(sc, w_ref[...].astype(BF16)) + b_ref[...]


def _ada_call(c_all, w_ada, b_ada, tn=768):
    depth, d, e = w_ada.shape
    rows = c_all.shape[0]
    return pl.pallas_call(
        _ada_kernel,
        grid=(depth, e // tn),
        in_specs=[
            pl.BlockSpec((rows, d), lambda l, j: (0, 0)),
            pl.BlockSpec((None, d, tn), lambda l, j: (l, 0, j)),
            pl.BlockSpec((None, 1, tn), lambda l, j: (l, 0, j)),
        ],
        out_specs=pl.BlockSpec((None, rows, tn), lambda l, j: (l, 0, j)),
        out_shape=jax.ShapeDtypeStruct((depth, rows, e), F32),
        compiler_params=_cparams("arbitrary", "arbitrary"),
        name="ada_mod",
    )(c_all, w_ada, b_ada.reshape(depth, 1, e))


def _bias_kernel(rel_ref, idx_ref, o_ref):
    h = pl.program_id(0)
    idx = idx_ref[...]
    acc = jnp.zeros(idx.shape, F32)
    for b in range(N_BUCKETS):
        acc = jnp.where(idx == b, rel_ref[b, h], acc)
    o_ref[...] = acc


def _bias_call(rel_bias, idx):
    r, c = idx.shape
    return pl.pallas_call(
        _bias_kernel,
        grid=(N_HEADS,),
        in_specs=[
            pl.BlockSpec(memory_space=pltpu.SMEM),
            pl.BlockSpec((r, c), lambda h: (0, 0)),
        ],
        out_specs=pl.BlockSpec((None, r, c), lambda h: (h, 0, 0)),
        out_shape=jax.ShapeDtypeStruct((N_HEADS, r, c), F32),
        compiler_params=_cparams("arbitrary"),
        name="rel_bias_tables",
    )(rel_bias, idx)


def _t5_bucket(n):
    max_exact = N_BUCKETS // 2
    large = max_exact + (jnp.log(jnp.maximum(n, 1).astype(F32) / max_exact)
                         / math.log(MAX_DISTANCE / max_exact) * (N_BUCKETS - max_exact)).astype(jnp.int32)
    large = jnp.minimum(large, N_BUCKETS - 1)
    return jnp.where(n < max_exact, n, large)


def _far_distance():
    n = np.arange(1, 4 * MAX_DISTANCE)
    large = 16 + (np.log(n.astype(np.float32) / 16) / math.log(MAX_DISTANCE / 16) * 16).astype(np.int32)
    bucket = np.where(n < 16, n, np.minimum(large, N_BUCKETS - 1))
    not_last = np.nonzero(bucket != N_BUCKETS - 1)[0]
    return int(n[not_last[-1]]) + 1


def _proj_kernel(*refs, layer, q_scales):
    n_kv = len(KV_GROUPS)
    x_ref, shift_ref, scale_ref, g_ref, w_ref, wf_ref, wft_ref, bfr_ref, bfc_ref = refs[:9]
    zb_ref, lf_ref, lft_ref = refs[9 + n_kv:12 + n_kv]
    kv_refs = refs[12 + n_kv:12 + 2 * n_kv]
    h_scr, stage, sem = refs[12 + 2 * n_kv:]
    i = pl.program_id(0)
    j = pl.program_id(1)
    tm = x_ref.shape[0]
    n_col = zb_ref.shape[1] // MXU_COLS
    heads_per_col = MXU_COLS // HEAD_DIM

    @pl.when(j == 0)
    def _():
        x = x_ref[...]
        ms = jnp.mean(x * x, axis=-1, keepdims=True)
        y = x * lax.rsqrt(ms + EPS) * g_ref[...]
        h = (y * (1.0 + scale_ref[...]) + shift_ref[...]).astype(BF16)
        h_scr[...] = h
        ff = _dot(h, wf_ref[...])[:, :N_HEADS] + bfr_ref[...]
        lf_ref[...] = _log_sigmoid(ff)
        fft = _dot_nt(wft_ref[...], h)[:N_HEADS, :] + bfc_ref[...]
        lft_ref[...] = _log_sigmoid(fft)

    def kv_copy(t, row0):
        return pltpu.make_async_copy(stage.at[t % 2], kv_refs[t].at[layer, pl.ds(row0, tm)], sem.at[t % 2])

    def emit(transform, kv_t=None):
        for c in range(n_col):
            cs = slice(c * MXU_COLS, (c + 1) * MXU_COLS)
            z = _dot(h_scr[...], w_ref[:, cs])
            zb_ref[:, cs] = transform(z).astype(BF16)
            if kv_t is not None:
                for hh in range(heads_per_col):
                    stage[kv_t % 2, :, c * heads_per_col + hh, :] = z[:, hh * HEAD_DIM:(hh + 1) * HEAD_DIM]

    def plain(z):
        return z

    @pl.when(j == G_DQ)
    def _():
        emit(lambda z: z * q_scales[0])

    @pl.when(j == G_FQ)
    def _():
        emit(lambda z: z * q_scales[1])

    @pl.when((j == G_DG) | (j == G_FG))
    def _():
        emit(_silu)

    for t, grp in enumerate(KV_GROUPS):
        @pl.when(j == grp)
        def _(t=t):
            if t >= 2:
                kv_copy(t - 2, 0).wait()
            else:
                @pl.when(i > 0)
                def _():
                    kv_copy(t + 2, 0).wait()
            emit(plain, t)
            kv_copy(t, i * tm).start()

    @pl.when((i == pl.num_programs(0) - 1) & (j == pl.num_programs(1) - 1))
    def _():
        kv_copy(2, 0).wait()
        kv_copy(3, 0).wait()


def _proj_call(x, shift, scale, g_pre, w_in_bf, wf_pad, wft_pad, bf_row, bf_col, kv_stacks, l, tm, name):
    m, d = x.shape
    groups, r, _ = shift.shape
    tiles_per_group = (m // tm) // groups
    width = (w_in_bf.shape[2] - N_HEADS) // 8
    nh = width // HEAD_DIM
    mod_spec = pl.BlockSpec((None, r, d), lambda i, j: (i // tiles_per_group, 0, 0))
    any_spec = pl.BlockSpec(memory_space=pl.ANY)
    n_kv = len(KV_GROUPS)
    kv_shape = jax.ShapeDtypeStruct(kv_stacks[0].shape, F32)
    q_scales = (DIFF_QK_DIM ** -0.5, HEAD_DIM ** -0.5)
    outs = pl.pallas_call(
        functools.partial(_proj_kernel, layer=l, q_scales=q_scales),
        grid=(m // tm, 8),
        in_specs=[
            pl.BlockSpec((tm, d), lambda i, j: (i, 0)),
            mod_spec, mod_spec,
            pl.BlockSpec((None, 1, d), lambda i, j: (l, 0, 0)),
            pl.BlockSpec((None, d, width), lambda i, j: (l, 0, j)),
            pl.BlockSpec((None, d, LANE), lambda i, j: (l, 0, 0)),
            pl.BlockSpec((None, 2 * SUBLANE, d), lambda i, j: (l, 0, 0)),
            pl.BlockSpec((None, 1, N_HEADS), lambda i, j: (l, 0, 0)),
            pl.BlockSpec((None, N_HEADS, 1), lambda i, j: (l, 0, 0)),
        ] + [any_spec] * n_kv,
        out_specs=[
            pl.BlockSpec((None, tm, width), lambda i, j: (j, i, 0)),
            pl.BlockSpec((tm, N_HEADS), lambda i, j: (i, 0)),
            pl.BlockSpec((N_HEADS, tm), lambda i, j: (0, i)),
        ] + [any_spec] * n_kv,
        out_shape=[
            jax.ShapeDtypeStruct((8, m, width), BF16),
            jax.ShapeDtypeStruct((m, N_HEADS), F32),
            jax.ShapeDtypeStruct((N_HEADS, m), F32),
        ] + [kv_shape] * n_kv,
        scratch_shapes=[
            pltpu.VMEM((tm, d), BF16),
            pltpu.VMEM((2, tm, nh, HEAD_DIM), F32),
            pltpu.SemaphoreType.DMA((2,)),
        ],
        input_output_aliases={9 + t: 3 + t for t in range(n_kv)},
        compiler_params=_cparams("arbitrary", "arbitrary"),
        name=name,
    )(x, shift, scale, g_pre, w_in_bf, wf_pad, wft_pad, bf_row, bf_col, *kv_stacks)
    return outs[0], outs[1], outs[2], list(outs[3:])


def _cum_kernel(x_ref, row_ref, col_ref):
    x = x_ref[...]
    t = x.shape[1]
    lane = lax.broadcasted_iota(jnp.int32, x.shape, 1)
    s = 1
    while s < t:
        x = x + jnp.where(lane >= s, pltpu.roll(x, s, 1), 0.0)
        s *= 2
    row_ref[...] = x
    xp = jnp.concatenate([x, jnp.zeros((LANE - x.shape[0], t), F32)], axis=0)
    col_ref[...] = xp.T


def _cum_call(logft, batch):
    nh, m = logft.shape
    t = m // batch
    return pl.pallas_call(
        _cum_kernel,
        grid=(batch,),
        in_specs=[pl.BlockSpec((nh, t), lambda b: (0, b))],
        out_specs=[
            pl.BlockSpec((None, nh, t), lambda b: (b, 0, 0)),
            pl.BlockSpec((None, t, LANE), lambda b: (b, 0, 0)),
        ],
        out_shape=[
            jax.ShapeDtypeStruct((batch, nh, t), F32),
            jax.ShapeDtypeStruct((batch, t, LANE), F32),
        ],
        compiler_params=_cparams("arbitrary"),
        name="prompt_logf_cumsum",
    )(logft)


def _lambda_full(lq1_ref, lk1_ref, lq2_ref, lk2_ref, lam_init):
    a = jnp.sum(lq1_ref[...] * lk1_ref[...], axis=-1, keepdims=True)
    b = jnp.sum(lq2_ref[...] * lk2_ref[...], axis=-1, keepdims=True)
    return jnp.exp(a) - jnp.exp(b) + lam_init


def _diff_rows(q):
    lane = lax.broadcasted_iota(jnp.int32, q.shape, 1)
    zero = jnp.zeros_like(q)
    return jnp.concatenate([jnp.where(lane < DIFF_QK_DIM, q, zero),
                            jnp.where(lane >= DIFF_QK_DIM, q, zero)], axis=0)


def _online_update(state, logits, shift_back, v):
    m, l, acc = state
    m_new = jnp.maximum(m, jnp.max(logits, axis=-1, keepdims=True) + shift_back)
    p = jnp.exp(logits - (m_new - shift_back))
    alpha = jnp.exp(m - m_new)
    l = alpha * l + jnp.sum(p, axis=-1, keepdims=True)
    acc = alpha * acc + _dot(p.astype(BF16), v)
    return m_new, l, acc


def _init_state(rows):
    return (jnp.full((rows, 1), NEG_BIG, F32), jnp.zeros((rows, 1), F32), jnp.zeros((rows, HEAD_DIM), F32))


def _diff_finish(acc, l, rows, lam, lam_init, subln):
    o = acc[:rows] / l[:rows] - lam * (acc[rows:] / l[rows:])
    return o * lax.rsqrt(jnp.mean(o * o, axis=-1, keepdims=True) + EPS) * subln * (1.0 - lam_init)


def _attn_kernel(rel_ref, dq_ref, dk_ref, dv_ref, dg_ref, fq_ref, fk_ref, fv_ref, fg_ref, bias_ref, ck_ref, cc_ref,
                 lq1_ref, lk1_ref, lq2_ref, lk2_ref, subln_ref, yd_ref, yf_ref, *, lam_init, blk):
    t = dq_ref.shape[0]
    nq = t // blk
    wide = 2 * blk
    h = pl.program_id(1)
    c_far = rel_ref[N_BUCKETS - 1, h]
    lam = _lambda_full(lq1_ref, lk1_ref, lq2_ref, lk2_ref, lam_init)
    row = lax.broadcasted_iota(jnp.int32, (blk, blk), 0)
    col = lax.broadcasted_iota(jnp.int32, (blk, blk), 1)
    causal = col <= row
    causal2 = jnp.concatenate([causal, causal], axis=0)
    head_lane = lax.broadcasted_iota(jnp.int32, (blk, LANE), 1) == h

    for qi in range(nq):
        qs = slice(qi * blk, (qi + 1) * blk)
        qd = _diff_rows(dq_ref[qs, :])
        qf = fq_ref[qs, :]
        cq = jnp.sum(jnp.where(head_lane, cc_ref[qs, :], 0.0), axis=-1, keepdims=True)
        sd, sf = _init_state(2 * blk), _init_state(blk)

        def far_step(ks, size, sd, sf, qd=qd, qf=qf, cq=cq):
            s = _dot_nt(qd, dk_ref[pl.ds(ks, size), :])
            sd = _online_update(sd, s, c_far, dv_ref[pl.ds(ks, size), :])
            s = _dot_nt(qf, fk_ref[pl.ds(ks, size), :]) - ck_ref[:, pl.ds(ks, size)]
            sf = _online_update(sf, s, cq, fv_ref[pl.ds(ks, size), :])
            return sd, sf

        n_far = max(qi - 1, 0)
        if n_far // 2:
            def pair_body(kk, st):
                return far_step(pl.multiple_of(kk * wide, wide), wide, *st)
            sd, sf = lax.fori_loop(0, n_far // 2, pair_body, (sd, sf))
        if n_far % 2:
            sd, sf = far_step((n_far - 1) * blk, blk, sd, sf)
        if qi >= 1:
            ks = (qi - 1) * blk
            b1 = bias_ref[1]
            s = _dot_nt(qd, dk_ref[ks:ks + blk, :]) + jnp.concatenate([b1, b1], axis=0)
            sd = _online_update(sd, s, 0.0, dv_ref[ks:ks + blk, :])
            s = _dot_nt(qf, fk_ref[ks:ks + blk, :]) - ck_ref[:, ks:ks + blk]
            sf = _online_update(sf, s, cq, fv_ref[ks:ks + blk, :])
        ks = qi * blk
        b0 = bias_ref[0]
        s = _dot_nt(qd, dk_ref[ks:ks + blk, :]) + jnp.concatenate([b0, b0], axis=0)
        s = jnp.where(causal2, s, NEG_BIG)
        _, l, acc = _online_update(sd, s, 0.0, dv_ref[ks:ks + blk, :])
        y = _diff_finish(acc, l, blk, lam, lam_init, subln_ref[...]) * dg_ref[qs, :].astype(F32)
        yd_ref[qs, :] = y.astype(BF16)
        s = _dot_nt(qf, fk_ref[ks:ks + blk, :]) - ck_ref[:, ks:ks + blk]
        s = jnp.where(causal, s, NEG_BIG)
        _, l, acc = _online_update(sf, s, cq, fv_ref[ks:ks + blk, :])
        yf_ref[qs, :] = (acc / l * fg_ref[qs, :].astype(F32)).astype(BF16)


def _attn_call(zb, bias_p, ck_rows, cum_cols, rel_bias, lq1, lk1, lq2, lk2, subln, l, batch, blk):
    _, m, width = zb.shape
    t = m // batch
    nh = width // HEAD_DIM

    def zspec(k):
        return pl.BlockSpec((None, t, HEAD_DIM), lambda b, h: (k, b, h))

    def pspec(n):
        return pl.BlockSpec((None, 1, n), lambda b, h: (l, 0, 0))

    y_spec = pl.BlockSpec((t, HEAD_DIM), lambda b, h: (b, h))
    y_shape = jax.ShapeDtypeStruct((m, width), BF16)
    return pl.pallas_call(
        functools.partial(_attn_kernel, lam_init=_lambda_init(l), blk=blk),
        grid=(batch, nh),
        in_specs=[
            pl.BlockSpec(memory_space=pltpu.SMEM),
            zspec(G_DQ), zspec(G_DK), zspec(G_DV), zspec(G_DG), zspec(G_FQ), zspec(G_FK), zspec(G_FV), zspec(G_FG),
            pl.BlockSpec((None, 2, blk, blk), lambda b, h: (h, 0, 0, 0)),
            pl.BlockSpec((None, 1, t), lambda b, h: (b * nh + h, 0, 0)),
            pl.BlockSpec((None, t, LANE), lambda b, h: (b, 0, 0)),
            pspec(DIFF_QK_DIM), pspec(DIFF_QK_DIM), pspec(DIFF_QK_DIM), pspec(DIFF_QK_DIM),
            pspec(HEAD_DIM),
        ],
        out_specs=[y_spec, y_spec],
        out_shape=[y_shape, y_shape],
        compiler_params=_cparams("arbitrary", "arbitrary"),
        name="prompt_attention",
    )(rel_bias, zb, zb, zb, zb, zb, zb, zb, zb, bias_p, ck_rows.reshape(batch * nh, 1, t), cum_cols,
      lq1, lk1, lq2, lk2, subln)


def _out_kernel(yd_ref, yf_ref, w_ref, x_ref, gate_ref, g_ref, o_ref):
    half = yd_ref.shape[1]
    y = _dot(yd_ref[...], w_ref[:half, :]) + _dot(yf_ref[...], w_ref[half:, :])
    r = y * lax.rsqrt(jnp.mean(y * y, axis=-1, keepdims=True) + EPS) * g_ref[...]
    o_ref[...] = x_ref[...] + gate_ref[...] * r


def _out_call(yd, yf, w_out_bf, x, gate, g_post, l, tm, name):
    m, d = x.shape
    half = yd.shape[1]
    groups, r, _ = gate.shape
    tiles_per_group = (m // tm) // groups
    return pl.pallas_call(
        _out_kernel,
        grid=(m // tm,),
        in_specs=[
            pl.BlockSpec((tm, half), lambda i: (i, 0)),
            pl.BlockSpec((tm, half), lambda i: (i, 0)),
            pl.BlockSpec((None, 2 * half, d), lambda i: (l, 0, 0)),
            pl.BlockSpec((tm, d), lambda i: (i, 0)),
            pl.BlockSpec((None, r, d), lambda i: (i // tiles_per_group, 0, 0)),
            pl.BlockSpec((None, 1, d), lambda i: (l, 0, 0)),
        ],
        out_specs=pl.BlockSpec((tm, d), lambda i: (i, 0)),
        out_shape=jax.ShapeDtypeStruct((m, d), F32),
        compiler_params=_cparams("arbitrary"),
        name=name,
    )(yd, yf, w_out_bf, x, gate, g_post)


def _dec_kernel(pt_ref, rel_ref, zs_ref, lfn_ref, bl_ref, bn_ref, lq1_ref, lk1_ref, lq2_ref, lk2_ref, subln_ref,
                cdk_ref, cdv_ref, cfk_ref, cfv_ref, clf_ref,
                yd_ref, yf_ref,
                kv_buf, lf_buf, sem, new_kv, tbl_d, tbl_f, cfar_scr,
                md_scr, ld_scr, accd_scr, mf_scr, lf_scr, accf_scr, carry_scr,
                *, layer, lam_init, n_chunks, n_batch, t_new):
    b = pl.program_id(0)
    cp = DEC_CHUNK_PAGES
    page = lf_buf.shape[3]
    prow = page * N_HEADS
    tcf = cp * prow
    rq = t_new * N_HEADS
    total = n_batch * n_chunks
    caches = (cdk_ref, cdv_ref, cfk_ref, cfv_ref)

    def chunk_copies(g, slot):
        gb = g // n_chunks
        gc = g % n_chunks
        out = []
        for j in range(cp):
            pg = pt_ref[gb, gc * cp + j]
            for ti, c_ref in enumerate(caches):
                out.append(pltpu.make_async_copy(c_ref.at[layer, pg], kv_buf.at[slot, ti, pl.ds(j * prow, prow), :],
                                                 sem.at[slot]))
            out.append(pltpu.make_async_copy(clf_ref.at[layer, pg], lf_buf.at[slot, j], sem.at[slot]))
        return out

    @pl.when(b == 0)
    def _():
        for g in range(DEC_SLOTS - 1):
            for c in chunk_copies(g, g):
                c.start()
        rr = lax.broadcasted_iota(jnp.int32, (2 * rq, tcf), 0) % N_HEADS
        cc = lax.broadcasted_iota(jnp.int32, (2 * rq, tcf), 1) % N_HEADS
        base = jnp.where(rr == cc, 0.0, NEG_BIG)
        row_head = lax.broadcasted_iota(jnp.int32, (2 * rq, 1), 0) % N_HEADS
        cfar = jnp.zeros((2 * rq, 1), F32)
        for h in range(N_HEADS):
            cfar = jnp.where(row_head == h, rel_ref[N_BUCKETS - 1, h], cfar)
        cfar_scr[...] = cfar
        tbl_d[0] = base
        tbl_d[1, :, :tcf - prow] = base[:, :tcf - prow]
        rr_p = lax.broadcasted_iota(jnp.int32, (2 * rq, prow), 0) % N_HEADS
        cc_p = lax.broadcasted_iota(jnp.int32, (2 * rq, prow), 1) % N_HEADS
        tbl_d[1, :, tcf - prow:] = jnp.where(rr_p == cc_p, bl_ref[...] - cfar_scr[...], NEG_BIG)
        tbl_f[...] = base[:rq]

    md_scr[...] = jnp.full(md_scr.shape, NEG_BIG, F32)
    ld_scr[...] = jnp.zeros(ld_scr.shape, F32)
    accd_scr[...] = jnp.zeros(accd_scr.shape, F32)
    mf_scr[...] = jnp.full(mf_scr.shape, NEG_BIG, F32)
    lf_scr[...] = jnp.zeros(lf_scr.shape, F32)
    accf_scr[...] = jnp.zeros(accf_scr.shape, F32)
    carry_scr[...] = jnp.zeros(carry_scr.shape, F32)

    qd = _diff_rows(zs_ref[G_DQ])
    qf = zs_ref[G_FQ]
    tri_r = lax.broadcasted_iota(jnp.int32, (page, page), 0)
    tri_c = lax.broadcasted_iota(jnp.int32, (page, page), 1)
    upper = (tri_r <= tri_c).astype(BF16)
    lane = lax.broadcasted_iota(jnp.int32, (N_HEADS, page), 1)
    spread = [(page // N_HEADS) * k + lane // N_HEADS for k in range(N_HEADS)]

    def chunk_body(c, _):
        g = b * n_chunks + c
        slot = g % DEC_SLOTS
        nxt = g + DEC_SLOTS - 1

        @pl.when(nxt < total)
        def _():
            for cpy in chunk_copies(nxt, nxt % DEC_SLOTS):
                cpy.start()

        for cpy in chunk_copies(g, slot):
            cpy.wait()

        local = []
        for j in range(cp):
            x = lf_buf[slot, j]
            hi = x.astype(BF16)
            r1 = x - hi.astype(F32)
            mid = r1.astype(BF16)
            lo = (r1 - mid.astype(F32)).astype(BF16)
            local.append(_dot(hi, upper) + _dot(mid, upper) + _dot(lo, upper))
        carry = carry_scr[...]
        parts = []
        for j in range(cp):
            ck = local[j] + carry
            carry = ck[:, page - 1:page]
            parts.extend(jnp.take_along_axis(ck, idx, axis=1) for idx in spread)
        carry_scr[...] = carry
        ck_cols = jnp.concatenate(parts, axis=1)
        ck_rows = jnp.concatenate([ck_cols] * t_new, axis=0)

        last = jnp.where(c == n_chunks - 1, 1, 0)
        s = _dot_nt(qd, kv_buf[slot, 0].astype(BF16)) + tbl_d[last]
        st = _online_update((md_scr[...], ld_scr[...], accd_scr[...]), s, cfar_scr[...], kv_buf[slot, 1].astype(BF16))
        md_scr[...], ld_scr[...], accd_scr[...] = st
        s = _dot_nt(qf, kv_buf[slot, 2].astype(BF16)) - ck_rows + tbl_f[...]
        st = _online_update((mf_scr[...], lf_scr[...], accf_scr[...]), s, 0.0, kv_buf[slot, 3].astype(BF16))
        mf_scr[...], lf_scr[...], accf_scr[...] = st
        return 0

    lax.fori_loop(0, n_chunks, chunk_body, 0)

    new_kv[...] = jnp.zeros(new_kv.shape, BF16)
    for ti, grp in enumerate(KV_GROUPS):
        new_kv[ti, 0:rq, :] = zs_ref[grp]
    lam = _lambda_full(lq1_ref, lk1_ref, lq2_ref, lk2_ref, lam_init)
    def new_token_mask(n_rows):
        rr = lax.broadcasted_iota(jnp.int32, (n_rows, LANE), 0)
        cc = lax.broadcasted_iota(jnp.int32, (n_rows, LANE), 1)
        return ((rr % N_HEADS) == (cc % N_HEADS)) & (cc // N_HEADS <= (rr % rq) // N_HEADS) & (cc < rq)

    s = _dot_nt(qd, new_kv[0]) + bn_ref[...]
    s = jnp.where(new_token_mask(2 * rq), s, NEG_BIG)
    _, l, acc = _online_update((md_scr[...], ld_scr[...], accd_scr[...]), s, 0.0, new_kv[1])
    y = _diff_finish(acc, l, rq, lam, lam_init, subln_ref[...]) * zs_ref[G_DG].astype(F32)
    yd_ref[...] = y.astype(BF16)

    lfn = lfn_ref[...]
    cnew = [carry_scr[...] + lfn[:, 0:1]]
    for j in range(1, t_new):
        cnew.append(cnew[-1] + lfn[:, j:j + 1])
    lane_n = lax.broadcasted_iota(jnp.int32, (N_HEADS, LANE), 1)
    ck_new = jnp.zeros((N_HEADS, LANE), F32)
    for j in range(t_new):
        ck_new = jnp.where(lane_n // N_HEADS == j, cnew[j], ck_new)
    ck_new_rows = jnp.concatenate([ck_new] * t_new, axis=0)
    cq = jnp.concatenate(cnew, axis=0)
    s = _dot_nt(qf, new_kv[2]) - ck_new_rows
    s = jnp.where(new_token_mask(rq), s, NEG_BIG)
    m_all = mf_scr[...] + cq
    m_new = jnp.maximum(m_all, jnp.max(s, axis=-1, keepdims=True) + cq)
    p = jnp.exp(s - (m_new - cq))
    alpha = jnp.exp(m_all - m_new)
    l_fin = alpha * lf_scr[...] + jnp.sum(p, axis=-1, keepdims=True)
    acc_fin = alpha * accf_scr[...] + _dot(p.astype(BF16), new_kv[3])
    yf_ref[...] = (acc_fin / l_fin * zs_ref[G_FG].astype(F32)).astype(BF16)


def _dec_call(page_table, rel_bias, zs, lfn, bias_last, bias_new, lq1, lk1, lq2, lk2, subln,
              cdk, cdv, cfk, cfv, clf, l, t_new):
    _, nb, rq, _ = zs.shape
    n_pages = page_table.shape[1]
    page = clf.shape[3]
    prow = page * N_HEADS
    n_chunks = n_pages // DEC_CHUNK_PAGES
    tcf = DEC_CHUNK_PAGES * prow

    def pspec(n):
        return pl.BlockSpec((None, 1, n), lambda b, pt: (l, 0, 0))

    any_spec = pl.BlockSpec(memory_space=pl.ANY)
    y_spec = pl.BlockSpec((None, rq, HEAD_DIM), lambda b, pt: (b, 0, 0))
    y_shape = jax.ShapeDtypeStruct((nb, rq, HEAD_DIM), BF16)
    grid_spec = pltpu.PrefetchScalarGridSpec(
        num_scalar_prefetch=1,
        grid=(nb,),
        in_specs=[
            pl.BlockSpec(memory_space=pltpu.SMEM),
            pl.BlockSpec((8, None, rq, HEAD_DIM), lambda b, pt: (0, b, 0, 0)),
            pl.BlockSpec((None, N_HEADS, t_new), lambda b, pt: (b, 0, 0)),
            pl.BlockSpec((2 * rq, prow), lambda b, pt: (0, 0)),
            pl.BlockSpec((2 * rq, LANE), lambda b, pt: (0, 0)),
            pspec(DIFF_QK_DIM), pspec(DIFF_QK_DIM), pspec(DIFF_QK_DIM), pspec(DIFF_QK_DIM),
            pspec(HEAD_DIM),
            any_spec, any_spec, any_spec, any_spec, any_spec,
        ],
        out_specs=[y_spec, y_spec],
        scratch_shapes=[
            pltpu.VMEM((DEC_SLOTS, 4, tcf, HEAD_DIM), F32),
            pltpu.VMEM((DEC_SLOTS, DEC_CHUNK_PAGES, N_HEADS, page), F32),
            pltpu.SemaphoreType.DMA((DEC_SLOTS,)),
            pltpu.VMEM((4, LANE, HEAD_DIM), BF16),
            pltpu.VMEM((2, 2 * rq, tcf), F32),
            pltpu.VMEM((rq, tcf), F32),
            pltpu.VMEM((2 * rq, 1), F32),
            pltpu.VMEM((2 * rq, 1), F32), pltpu.VMEM((2 * rq, 1), F32), pltpu.VMEM((2 * rq, HEAD_DIM), F32),
            pltpu.VMEM((rq, 1), F32), pltpu.VMEM((rq, 1), F32), pltpu.VMEM((rq, HEAD_DIM), F32),
            pltpu.VMEM((N_HEADS, 1), F32),
        ],
    )
    return pl.pallas_call(
        functools.partial(_dec_kernel, layer=l, lam_init=_lambda_init(l), n_chunks=n_chunks, n_batch=nb, t_new=t_new),
        grid_spec=grid_spec,
        out_shape=[y_shape, y_shape],
        compiler_params=_cparams("arbitrary"),
        name="sample_paged_attention",
    )(page_table, rel_bias, zs, lfn, bias_last, bias_new, lq1, lk1, lq2, lk2, subln, cdk, cdv, cfk, cfv, clf)


def kernel(x_prompt, x_sample, c_prompt, c_sample, cache_diff_k, cache_diff_v, cache_fox_k, cache_fox_v, cache_fox_logf, page_table, w_ada, b_ada, norm_pre, norm_post, w_in, b_forget, lambda_q1, lambda_k1, lambda_q2, lambda_k2, diff_subln, rel_bias, w_out):
    batch, seq, d = x_prompt.shape
    nb, t_new, _ = x_sample.shape
    depth = w_ada.shape[0]
    n_pool, page = cache_diff_k.shape[1], cache_diff_k.shape[2]
    width = N_HEADS * HEAD_DIM
    blk = ATTN_BLOCK
    m_p, m_s = batch * seq, nb * t_new
    rq = t_new * N_HEADS
    prow = page * N_HEADS

    far = _far_distance()
    assert far <= blk + 1 and far <= page + 1 and page == LANE and rq <= LANE

    w_in_bf = w_in.astype(BF16)
    wf = w_in[:, :, 8 * width:]
    wf_pad = jnp.pad(wf, ((0, 0), (0, 0), (0, LANE - N_HEADS))).astype(BF16)
    wft_pad = jnp.pad(jnp.swapaxes(wf, 1, 2), ((0, 0), (0, 2 * SUBLANE - N_HEADS), (0, 0))).astype(BF16)
    w_out_bf = w_out.astype(BF16)
    bf_row = b_forget.reshape(depth, 1, N_HEADS)
    bf_col = b_forget.reshape(depth, N_HEADS, 1)
    g_pre = norm_pre.reshape(depth, 1, d)
    g_post = norm_post.reshape(depth, 1, d)
    lq1 = lambda_q1.reshape(depth, 1, DIFF_QK_DIM)
    lk1 = lambda_k1.reshape(depth, 1, DIFF_QK_DIM)
    lq2 = lambda_q2.reshape(depth, 1, DIFF_QK_DIM)
    lk2 = lambda_k2.reshape(depth, 1, DIFF_QK_DIM)
    subln = diff_subln.reshape(depth, 1, HEAD_DIM)

    rows = -(-(batch + nb) // (2 * SUBLANE)) * (2 * SUBLANE)
    c_all = jnp.concatenate([c_prompt, c_sample, jnp.zeros((rows - batch - nb, d), F32)], axis=0)
    mod = _ada_call(c_all, w_ada, b_ada)

    r_i = jnp.arange(blk, dtype=jnp.int32)
    dist = r_i[:, None] - r_i[None, :]
    idx_p = jnp.concatenate([_t5_bucket(jnp.maximum(dist, 0)), _t5_bucket(dist + blk)], axis=0)
    bias_p = _bias_call(rel_bias, idx_p).reshape(N_HEADS, 2, blk, blk)
    tq = jnp.arange(2 * t_new, dtype=jnp.int32)[:, None] % t_new
    pos_last = jnp.arange(prow, dtype=jnp.int32)[None, :] // N_HEADS
    pos_new = jnp.arange(LANE, dtype=jnp.int32)[None, :] // N_HEADS
    idx_dec = jnp.concatenate([_t5_bucket(tq + page - pos_last), _t5_bucket(jnp.maximum(tq - pos_new, 0))], axis=1)
    bias_dec = _bias_call(rel_bias, idx_dec)
    bias_dec = jnp.transpose(bias_dec.reshape(N_HEADS, 2, t_new, prow + LANE), (1, 2, 0, 3)).reshape(2 * rq, prow + LANE)
    bias_last, bias_new = bias_dec[:, :prow], bias_dec[:, prow:]

    cdk = cache_diff_k.reshape(depth, n_pool, prow, HEAD_DIM)
    cdv = cache_diff_v.reshape(depth, n_pool, prow, HEAD_DIM)
    cfk = cache_fox_k.reshape(depth, n_pool, prow, HEAD_DIM)
    cfv = cache_fox_v.reshape(depth, n_pool, prow, HEAD_DIM)
    clf = jnp.swapaxes(cache_fox_logf, 2, 3)

    xp = x_prompt.reshape(m_p, d)
    xs = x_sample.reshape(m_s, d)
    tm_p = min(1024, seq)
    kv_p = [jnp.zeros((depth, m_p, N_HEADS, HEAD_DIM), F32) for _ in KV_GROUPS]
    kv_s = [jnp.zeros((depth, m_s, N_HEADS, HEAD_DIM), F32) for _ in KV_GROUPS]
    logf_p, logf_s = [], []
    for l in range(depth):
        mp = mod[l, :batch]
        ms = jnp.repeat(mod[l, batch:batch + nb], t_new, axis=0)
        shift_p, scale_p, gate_p = (mp[:, None, k * d:(k + 1) * d] for k in range(3))
        shift_s, scale_s, gate_s = (ms[None, :, k * d:(k + 1) * d] for k in range(3))

        zb, logf, logft, kv_p = _proj_call(xp, shift_p, scale_p, g_pre, w_in_bf, wf_pad, wft_pad, bf_row, bf_col,
                                           kv_p, l, tm_p, "prompt_in_proj")
        ck_rows, cum_cols = _cum_call(logft, batch)
        yd, yf = _attn_call(zb, bias_p, ck_rows, cum_cols, rel_bias, lq1, lk1, lq2, lk2, subln, l, batch, blk)
        xp = _out_call(yd, yf, w_out_bf, xp, gate_p, g_post, l, min(512, seq), "prompt_out_proj")
        logf_p.append(logf.reshape(batch, seq, N_HEADS))

        zs, logf_n, logft_n, kv_s = _proj_call(xs, shift_s, scale_s, g_pre, w_in_bf, wf_pad, wft_pad, bf_row, bf_col,
                                               kv_s, l, m_s, "sample_in_proj")
        lfn = jnp.transpose(logft_n.reshape(N_HEADS, nb, t_new), (1, 0, 2))
        yds, yfs = _dec_call(page_table, rel_bias, zs.reshape(8, nb, rq, HEAD_DIM), lfn, bias_last, bias_new,
                             lq1, lk1, lq2, lk2, subln, cdk, cdv, cfk, cfv, clf, l, t_new)
        xs = _out_call(yds.reshape(m_s, width), yfs.reshape(m_s, width), w_out_bf, xs, gate_s, g_post,
                       l, m_s, "sample_out_proj")
        logf_s.append(logf_n.reshape(nb, t_new, N_HEADS))

    outs_p = [a.reshape(depth, batch, seq, N_HEADS, HEAD_DIM) for a in kv_p]
    outs_s = [a.reshape(depth, nb, t_new, N_HEADS, HEAD_DIM) for a in kv_s]
    return (xp.reshape(batch, seq, d), xs.reshape(nb, t_new, d), *outs_p, jnp.stack(logf_p, axis=0),
            *outs_s, jnp.stack(logf_s, axis=0))
```

```python
import functools
import math

import numpy as np
import jax
import jax.numpy as jnp
from jax import lax
from jax.experimental import pallas as pl
from jax.experimental.pallas import tpu as pltpu

F32 = jnp.float32
BF16 = jnp.bfloat16

EPS = 1e-6
HEAD_DIM = 128
DIFF_QK_DIM = 64
N_HEADS = 8
N_BUCKETS = 32
MAX_DISTANCE = 128
NEG_BIG = -1e30
LOG2E = math.log2(math.e)

V7X_VMEM_LIMIT_BYTES = 56 * 1024 * 1024
LANE = 128
SUBLANE = 8
MXU_COLS = 256

ATTN_BLOCK = 256
DEC_CHUNK_PAGES = 4
DEC_SLOTS = 3

G_DQ, G_DK, G_DV, G_DG, G_FQ, G_FK, G_FV, G_FG = range(8)
KV_GROUPS = (G_DK, G_DV, G_FK, G_FV)


def _cparams(*sem):
    return pltpu.CompilerParams(dimension_semantics=sem, vmem_limit_bytes=V7X_VMEM_LIMIT_BYTES)


def _lambda_init(l):
    return 0.8 - 0.6 * math.exp(-0.3 * l)


def _log_sigmoid(x):
    return jnp.minimum(x, 0.0) - jnp.log1p(jnp.exp(-jnp.abs(x)))


def _silu(x):
    return x * jax.nn.sigmoid(x)


def _dot(a, b):
    return jnp.dot(a, b, preferred_element_type=F32)


def _dot_nt(a, b):
    return lax.dot_general(a, b, (((1,), (1,)), ((), ())), preferred_element_type=F32)


def _ada_kernel(c_ref, w_ref, b_ref, o_ref):
    sc = _silu(c_ref[...]).astype(BF16)
    o_ref[...] = _dot(sc, w_ref[...].astype(BF16)) + b_ref[...]


def _ada_call(c_all, w_ada, b_ada, tn=768):
    depth, d, e = w_ada.shape
    rows = c_all.shape[0]
    return pl.pallas_call(
        _ada_kernel,
        grid=(depth, e // tn),
        in_specs=[
            pl.BlockSpec((rows, d), lambda l, j: (0, 0)),
            pl.BlockSpec((None, d, tn), lambda l, j: (l, 0, j)),
            pl.BlockSpec((None, 1, tn), lambda l, j: (l, 0, j)),
        ],
        out_specs=pl.BlockSpec((None, rows, tn), lambda l, j: (l, 0, j)),
        out_shape=jax.ShapeDtypeStruct((depth, rows, e), F32),
        compiler_params=_cparams("arbitrary", "arbitrary"),
        name="ada_mod",
    )(c_all, w_ada, b_ada.reshape(depth, 1, e))


def _bias_kernel(rel_ref, idx_ref, o_ref, *, scale):
    h = pl.program_id(0)
    idx = idx_ref[...]
    acc = jnp.zeros(idx.shape, F32)
    for b in range(N_BUCKETS):
        acc = jnp.where(idx == b, rel_ref[b, h], acc)
    o_ref[...] = acc * scale


def _bias_call(rel_bias, idx, scale=1.0):
    r, c = idx.shape
    return pl.pallas_call(
        functools.partial(_bias_kernel, scale=scale),
        grid=(N_HEADS,),
        in_specs=[
            pl.BlockSpec(memory_space=pltpu.SMEM),
            pl.BlockSpec((r, c), lambda h: (0, 0)),
        ],
        out_specs=pl.BlockSpec((None, r, c), lambda h: (h, 0, 0)),
        out_shape=jax.ShapeDtypeStruct((N_HEADS, r, c), F32),
        compiler_params=_cparams("arbitrary"),
        name="rel_bias_tables",
    )(rel_bias, idx)


def _t5_bucket(n):
    max_exact = N_BUCKETS // 2
    large = max_exact + (jnp.log(jnp.maximum(n, 1).astype(F32) / max_exact)
                         / math.log(MAX_DISTANCE / max_exact) * (N_BUCKETS - max_exact)).astype(jnp.int32)
    large = jnp.minimum(large, N_BUCKETS - 1)
    return jnp.where(n < max_exact, n, large)


def _far_distance():
    n = np.arange(1, 4 * MAX_DISTANCE)
    large = 16 + (np.log(n.astype(np.float32) / 16) / math.log(MAX_DISTANCE / 16) * 16).astype(np.int32)
    bucket = np.where(n < 16, n, np.minimum(large, N_BUCKETS - 1))
    not_last = np.nonzero(bucket != N_BUCKETS - 1)[0]
    return int(n[not_last[-1]]) + 1


def _proj_kernel(*refs, layer, q_scales):
    n_kv = len(KV_GROUPS)
    x_ref, shift_ref, scale_ref, g_ref, w_ref, wf_ref, wft_ref, bfr_ref, bfc_ref = refs[:9]
    zb_ref, lf_ref, lft_ref = refs[9 + n_kv:12 + n_kv]
    kv_refs = refs[12 + n_kv:12 + 2 * n_kv]
    h_scr, stage, sem = refs[12 + 2 * n_kv:]
    i = pl.program_id(0)
    j = pl.program_id(1)
    tm = x_ref.shape[0]
    n_col = zb_ref.shape[1] // MXU_COLS
    n_heads = zb_ref.shape[1] // HEAD_DIM

    @pl.when(j == 0)
    def _():
        x = x_ref[...]
        ms = jnp.mean(x * x, axis=-1, keepdims=True)
        y = x * lax.rsqrt(ms + EPS) * g_ref[...]
        h = (y * (1.0 + scale_ref[...]) + shift_ref[...]).astype(BF16)
        h_scr[...] = h
        ff = _dot(h, wf_ref[...])[:, :N_HEADS] + bfr_ref[...]
        lf_ref[...] = _log_sigmoid(ff)
        fft = _dot_nt(wft_ref[...], h)[:N_HEADS, :] + bfc_ref[...]
        lft_ref[...] = _log_sigmoid(fft)

    def kv_copies(t, row0):
        return [pltpu.make_async_copy(stage.at[t % 2, :, pl.ds(hh * HEAD_DIM, HEAD_DIM)],
                                      kv_refs[t].at[layer, pl.ds(row0, tm), hh, :], sem.at[t % 2])
                for hh in range(n_heads)]

    def emit(transform, kv_t=None):
        for c in range(n_col):
            cs = slice(c * MXU_COLS, (c + 1) * MXU_COLS)
            z = _dot(h_scr[...], w_ref[:, cs])
            zb_ref[:, cs] = transform(z).astype(BF16)
            if kv_t is not None:
                stage[kv_t % 2, :, cs] = z

    def plain(z):
        return z

    @pl.when(j == G_DQ)
    def _():
        emit(lambda z: z * q_scales[0])

    @pl.when(j == G_FQ)
    def _():
        emit(lambda z: z * q_scales[1])

    @pl.when((j == G_DG) | (j == G_FG))
    def _():
        emit(_silu)

    for t, grp in enumerate(KV_GROUPS):
        @pl.when(j == grp)
        def _(t=t):
            if t >= 2:
                for cp in kv_copies(t - 2, 0):
                    cp.wait()
            else:
                @pl.when(i > 0)
                def _():
                    for cp in kv_copies(t + 2, 0):
                        cp.wait()
            emit(plain, t)
            for cp in kv_copies(t, i * tm):
                cp.start()

    @pl.when((i == pl.num_programs(0) - 1) & (j == pl.num_programs(1) - 1))
    def _():
        for cp in kv_copies(2, 0) + kv_copies(3, 0):
            cp.wait()


def _proj_call(x, shift, scale, g_pre, w_in_bf, wf_pad, wft_pad, bf_row, bf_col, kv_stacks, l, tm, logit_scale, name):
    m, d = x.shape
    groups, r, _ = shift.shape
    tiles_per_group = (m // tm) // groups
    width = (w_in_bf.shape[2] - N_HEADS) // 8
    nh = width // HEAD_DIM
    mod_spec = pl.BlockSpec((None, r, d), lambda i, j: (i // tiles_per_group, 0, 0))
    any_spec = pl.BlockSpec(memory_space=pl.ANY)
    n_kv = len(KV_GROUPS)
    kv_shape = jax.ShapeDtypeStruct(kv_stacks[0].shape, F32)
    q_scales = (DIFF_QK_DIM ** -0.5 * logit_scale, HEAD_DIM ** -0.5 * logit_scale)
    outs = pl.pallas_call(
        functools.partial(_proj_kernel, layer=l, q_scales=q_scales),
        grid=(m // tm, 8),
        in_specs=[
            pl.BlockSpec((tm, d), lambda i, j: (i, 0)),
            mod_spec, mod_spec,
            pl.BlockSpec((None, 1, d), lambda i, j: (l, 0, 0)),
            pl.BlockSpec((None, d, width), lambda i, j: (l, 0, j)),
            pl.BlockSpec((None, d, LANE), lambda i, j: (l, 0, 0)),
            pl.BlockSpec((None, 2 * SUBLANE, d), lambda i, j: (l, 0, 0)),
            pl.BlockSpec((None, 1, N_HEADS), lambda i, j: (l, 0, 0)),
            pl.BlockSpec((None, N_HEADS, 1), lambda i, j: (l, 0, 0)),
        ] + [any_spec] * n_kv,
        out_specs=[
            pl.BlockSpec((None, tm, width), lambda i, j: (j, i, 0)),
            pl.BlockSpec((tm, N_HEADS), lambda i, j: (i, 0)),
            pl.BlockSpec((N_HEADS, tm), lambda i, j: (0, i)),
        ] + [any_spec] * n_kv,
        out_shape=[
            jax.ShapeDtypeStruct((8, m, width), BF16),
            jax.ShapeDtypeStruct((m, N_HEADS), F32),
            jax.ShapeDtypeStruct((N_HEADS, m), F32),
        ] + [kv_shape] * n_kv,
        scratch_shapes=[
            pltpu.VMEM((tm, d), BF16),
            pltpu.VMEM((2, tm, width), F32),
            pltpu.SemaphoreType.DMA((2,)),
        ],
        input_output_aliases={9 + t: 3 + t for t in range(n_kv)},
        compiler_params=_cparams("arbitrary", "arbitrary"),
        name=name,
    )(x, shift, scale, g_pre, w_in_bf, wf_pad, wft_pad, bf_row, bf_col, *kv_stacks)
    return outs[0], outs[1], outs[2], list(outs[3:])


def _cum_kernel(x_ref, row_ref, col_ref):
    x = x_ref[...]
    t = x.shape[1]
    lane = lax.broadcasted_iota(jnp.int32, x.shape, 1)
    s = 1
    while s < t:
        x = x + jnp.where(lane >= s, pltpu.roll(x, s, 1), 0.0)
        s *= 2
    x = x * LOG2E
    row_ref[...] = x
    xp = jnp.concatenate([x, jnp.zeros((LANE - x.shape[0], t), F32)], axis=0)
    col_ref[...] = xp.T


def _cum_call(logft, batch):
    nh, m = logft.shape
    t = m // batch
    return pl.pallas_call(
        _cum_kernel,
        grid=(batch,),
        in_specs=[pl.BlockSpec((nh, t), lambda b: (0, b))],
        out_specs=[
            pl.BlockSpec((None, nh, t), lambda b: (b, 0, 0)),
            pl.BlockSpec((None, t, LANE), lambda b: (b, 0, 0)),
        ],
        out_shape=[
            jax.ShapeDtypeStruct((batch, nh, t), F32),
            jax.ShapeDtypeStruct((batch, t, LANE), F32),
        ],
        compiler_params=_cparams("arbitrary"),
        name="prompt_logf_cumsum",
    )(logft)


def _lambda_full(lq1_ref, lk1_ref, lq2_ref, lk2_ref, lam_init):
    a = jnp.sum(lq1_ref[...] * lk1_ref[...], axis=-1, keepdims=True)
    b = jnp.sum(lq2_ref[...] * lk2_ref[...], axis=-1, keepdims=True)
    return jnp.exp(a) - jnp.exp(b) + lam_init


def _diff_rows(q):
    lane = lax.broadcasted_iota(jnp.int32, q.shape, 1)
    zero = jnp.zeros_like(q)
    return jnp.concatenate([jnp.where(lane < DIFF_QK_DIM, q, zero),
                            jnp.where(lane >= DIFF_QK_DIM, q, zero)], axis=0)


def _online_update(state, logits, shift_back, v, exp=jnp.exp):
    m, l, acc = state
    m_new = jnp.maximum(m, jnp.max(logits, axis=-1, keepdims=True) + shift_back)
    p = exp(logits - (m_new - shift_back))
    alpha = exp(m - m_new)
    l = alpha * l + jnp.sum(p, axis=-1, keepdims=True)
    acc = alpha * acc + _dot(p.astype(BF16), v)
    return m_new, l, acc


def _init_state(rows):
    return (jnp.full((rows, 1), NEG_BIG, F32), jnp.zeros((rows, 1), F32), jnp.zeros((rows, HEAD_DIM), F32))


def _diff_finish(acc, l, rows, lam, lam_init, subln):
    o = acc[:rows] / l[:rows] - lam * (acc[rows:] / l[rows:])
    return o * lax.rsqrt(jnp.mean(o * o, axis=-1, keepdims=True) + EPS) * subln * (1.0 - lam_init)


def _attn_kernel(rel_ref, dq_ref, dk_ref, dv_ref, dg_ref, fq_ref, fk_ref, fv_ref, fg_ref, bias_ref, ck_ref, cc_ref,
                 lq1_ref, lk1_ref, lq2_ref, lk2_ref, subln_ref, yd_ref, yf_ref, *, lam_init, blk):
    t = dq_ref.shape[0]
    nq = t // blk
    h = pl.program_id(1)
    c_far = rel_ref[N_BUCKETS - 1, h] * LOG2E
    lam = _lambda_full(lq1_ref, lk1_ref, lq2_ref, lk2_ref, lam_init)
    row = lax.broadcasted_iota(jnp.int32, (blk, blk), 0)
    col = lax.broadcasted_iota(jnp.int32, (blk, blk), 1)
    causal = col <= row
    bias_d = jnp.where(causal, bias_ref[0], NEG_BIG)
    mask_d = jnp.where(causal, 0.0, NEG_BIG)
    bias_nd = jnp.concatenate([bias_ref[1], bias_d], axis=1)
    mask_nd = jnp.concatenate([jnp.zeros((blk, blk), F32), mask_d], axis=1)
    bias_d2 = jnp.concatenate([bias_d, bias_d], axis=0)
    bias_nd2 = jnp.concatenate([bias_nd, bias_nd], axis=0)
    head_lane = lax.broadcasted_iota(jnp.int32, (blk, LANE), 1) == h
    upd = functools.partial(_online_update, exp=jnp.exp2)

    for qi in range(nq):
        qs = slice(qi * blk, (qi + 1) * blk)
        qd = _diff_rows(dq_ref[qs, :])
        qf = fq_ref[qs, :]
        cq = jnp.sum(jnp.where(head_lane, cc_ref[qs, :], 0.0), axis=-1, keepdims=True)
        sd, sf = _init_state(2 * blk), _init_state(blk)
        far_w = max(qi - 1, 0) * blk
        if far_w:
            s = _dot_nt(qd, dk_ref[0:far_w, :])
            sd = upd(sd, s, c_far, dv_ref[0:far_w, :])
            s = _dot_nt(qf, fk_ref[0:far_w, :]) - ck_ref[:, 0:far_w]
            sf = upd(sf, s, cq, fv_ref[0:far_w, :])
        ks = far_w
        ke = (qi + 1) * blk
        s = _dot_nt(qd, dk_ref[ks:ke, :]) + (bias_nd2 if qi else bias_d2)
        _, l, acc = upd(sd, s, 0.0, dv_ref[ks:ke, :])
        y = _diff_finish(acc, l, blk, lam, lam_init, subln_ref[...]) * dg_ref[qs, :].astype(F32)
        yd_ref[qs, :] = y.astype(BF16)
        s = _dot_nt(qf, fk_ref[ks:ke, :]) - ck_ref[:, ks:ke] + (mask_nd if qi else mask_d)
        _, l, acc = upd(sf, s, cq, fv_ref[ks:ke, :])
        yf_ref[qs, :] = (acc / l * fg_ref[qs, :].astype(F32)).astype(BF16)


def _attn_call(zb, bias_p, ck_rows, cum_cols, rel_bias, lq1, lk1, lq2, lk2, subln, l, batch, blk):
    _, m, width = zb.shape
    t = m // batch
    nh = width // HEAD_DIM

    def zspec(k):
        return pl.BlockSpec((None, t, HEAD_DIM), lambda b, h: (k, b, h))

    def pspec(n):
        return pl.BlockSpec((None, 1, n), lambda b, h: (l, 0, 0))

    y_spec = pl.BlockSpec((t, HEAD_DIM), lambda b, h: (b, h))
    y_shape = jax.ShapeDtypeStruct((m, width), BF16)
    return pl.pallas_call(
        functools.partial(_attn_kernel, lam_init=_lambda_init(l), blk=blk),
        grid=(batch, nh),
        in_specs=[
            pl.BlockSpec(memory_space=pltpu.SMEM),
            zspec(G_DQ), zspec(G_DK), zspec(G_DV), zspec(G_DG), zspec(G_FQ), zspec(G_FK), zspec(G_FV), zspec(G_FG),
            pl.BlockSpec((None, 2, blk, blk), lambda b, h: (h, 0, 0, 0)),
            pl.BlockSpec((None, 1, t), lambda b, h: (b * nh + h, 0, 0)),
            pl.BlockSpec((None, t, LANE), lambda b, h: (b, 0, 0)),
            pspec(DIFF_QK_DIM), pspec(DIFF_QK_DIM), pspec(DIFF_QK_DIM), pspec(DIFF_QK_DIM),
            pspec(HEAD_DIM),
        ],
        out_specs=[y_spec, y_spec],
        out_shape=[y_shape, y_shape],
        compiler_params=_cparams("arbitrary", "arbitrary"),
        name="prompt_attention",
    )(rel_bias, zb, zb, zb, zb, zb, zb, zb, zb, bias_p, ck_rows.reshape(batch * nh, 1, t), cum_cols,
      lq1, lk1, lq2, lk2, subln)


def _out_kernel(yd_ref, yf_ref, w_ref, x_ref, gate_ref, g_ref, o_ref):
    half = yd_ref.shape[1]
    y = _dot(yd_ref[...], w_ref[:half, :]) + _dot(yf_ref[...], w_ref[half:, :])
    r = y * lax.rsqrt(jnp.mean(y * y, axis=-1, keepdims=True) + EPS) * g_ref[...]
    o_ref[...] = x_ref[...] + gate_ref[...] * r


def _out_call(yd, yf, w_out_bf, x, gate, g_post, l, tm, name):
    m, d = x.shape
    half = yd.shape[1]
    groups, r, _ = gate.shape
    tiles_per_group = (m // tm) // groups
    return pl.pallas_call(
        _out_kernel,
        grid=(m // tm,),
        in_specs=[
            pl.BlockSpec((tm, half), lambda i: (i, 0)),
            pl.BlockSpec((tm, half), lambda i: (i, 0)),
            pl.BlockSpec((None, 2 * half, d), lambda i: (l, 0, 0)),
            pl.BlockSpec((tm, d), lambda i: (i, 0)),
            pl.BlockSpec((None, r, d), lambda i: (i // tiles_per_group, 0, 0)),
            pl.BlockSpec((None, 1, d), lambda i: (l, 0, 0)),
        ],
        out_specs=pl.BlockSpec((tm, d), lambda i: (i, 0)),
        out_shape=jax.ShapeDtypeStruct((m, d), F32),
        compiler_params=_cparams("arbitrary"),
        name=name,
    )(yd, yf, w_out_bf, x, gate, g_post)


def _dec_kernel(pt_ref, rel_ref, zs_ref, lfn_ref, bl_ref, bn_ref, lq1_ref, lk1_ref, lq2_ref, lk2_ref, subln_ref,
                cdk_ref, cdv_ref, cfk_ref, cfv_ref, clf_ref,
                yd_ref, yf_ref,
                kv_buf, lf_buf, sem, new_kv, tbl_d, tbl_f, cfar_scr,
                md_scr, ld_scr, accd_scr, mf_scr, lf_scr, accf_scr, carry_scr,
                *, layer, lam_init, n_chunks, n_batch, t_new):
    b = pl.program_id(0)
    cp = DEC_CHUNK_PAGES
    page = lf_buf.shape[3]
    prow = page * N_HEADS
    tcf = cp * prow
    rq = t_new * N_HEADS
    total = n_batch * n_chunks
    caches = (cdk_ref, cdv_ref, cfk_ref, cfv_ref)

    def chunk_copies(g, slot):
        gb = g // n_chunks
        gc = g % n_chunks
        out = []
        for j in range(cp):
            pg = pt_ref[gb, gc * cp + j]
            for ti, c_ref in enumerate(caches):
                out.append(pltpu.make_async_copy(c_ref.at[layer, pg], kv_buf.at[slot, ti, pl.ds(j * prow, prow), :],
                                                 sem.at[slot]))
            out.append(pltpu.make_async_copy(clf_ref.at[layer, pg], lf_buf.at[slot, j], sem.at[slot]))
        return out

    @pl.when(b == 0)
    def _():
        for g in range(DEC_SLOTS - 1):
            for c in chunk_copies(g, g):
                c.start()
        rr = lax.broadcasted_iota(jnp.int32, (2 * rq, tcf), 0) % N_HEADS
        cc = lax.broadcasted_iota(jnp.int32, (2 * rq, tcf), 1) % N_HEADS
        base = jnp.where(rr == cc, 0.0, NEG_BIG)
        row_head = lax.broadcasted_iota(jnp.int32, (2 * rq, 1), 0) % N_HEADS
        cfar = jnp.zeros((2 * rq, 1), F32)
        for h in range(N_HEADS):
            cfar = jnp.where(row_head == h, rel_ref[N_BUCKETS - 1, h], cfar)
        cfar_scr[...] = cfar
        tbl_d[0] = base
        tbl_d[1, :, :tcf - prow] = base[:, :tcf - prow]
        rr_p = lax.broadcasted_iota(jnp.int32, (2 * rq, prow), 0) % N_HEADS
        cc_p = lax.broadcasted_iota(jnp.int32, (2 * rq, prow), 1) % N_HEADS
        tbl_d[1, :, tcf - prow:] = jnp.where(rr_p == cc_p, bl_ref[...] - cfar_scr[...], NEG_BIG)
        tbl_f[...] = base[:rq]

    md_scr[...] = jnp.full(md_scr.shape, NEG_BIG, F32)
    ld_scr[...] = jnp.zeros(ld_scr.shape, F32)
    accd_scr[...] = jnp.zeros(accd_scr.shape, F32)
    mf_scr[...] = jnp.full(mf_scr.shape, NEG_BIG, F32)
    lf_scr[...] = jnp.zeros(lf_scr.shape, F32)
    accf_scr[...] = jnp.zeros(accf_scr.shape, F32)
    carry_scr[...] = jnp.zeros(carry_scr.shape, F32)

    qd = _diff_rows(zs_ref[G_DQ])
    qf = zs_ref[G_FQ]
    tri_r = lax.broadcasted_iota(jnp.int32, (page, page), 0)
    tri_c = lax.broadcasted_iota(jnp.int32, (page, page), 1)
    upper = (tri_r <= tri_c).astype(BF16)
    lane = lax.broadcasted_iota(jnp.int32, (N_HEADS, page), 1)
    spread = [(page // N_HEADS) * k + lane // N_HEADS for k in range(N_HEADS)]

    def chunk_body(c, _):
        g = b * n_chunks + c
        slot = g % DEC_SLOTS
        nxt = g + DEC_SLOTS - 1

        @pl.when(nxt < total)
        def _():
            for cpy in chunk_copies(nxt, nxt % DEC_SLOTS):
                cpy.start()

        for cpy in chunk_copies(g, slot):
            cpy.wait()

        local = []
        for j in range(cp):
            x = lf_buf[slot, j]
            hi = x.astype(BF16)
            r1 = x - hi.astype(F32)
            mid = r1.astype(BF16)
            lo = (r1 - mid.astype(F32)).astype(BF16)
            local.append(_dot(hi, upper) + _dot(mid, upper) + _dot(lo, upper))
        carry = carry_scr[...]
        parts = []
        for j in range(cp):
            ck = local[j] + carry
            carry = ck[:, page - 1:page]
            parts.extend(jnp.take_along_axis(ck, idx, axis=1) for idx in spread)
        carry_scr[...] = carry
        ck_cols = jnp.concatenate(parts, axis=1)
        ck_rows = jnp.concatenate([ck_cols] * t_new, axis=0)

        last = jnp.where(c == n_chunks - 1, 1, 0)
        s = _dot_nt(qd, kv_buf[slot, 0].astype(BF16)) + tbl_d[last]
        st = _online_update((md_scr[...], ld_scr[...], accd_scr[...]), s, cfar_scr[...], kv_buf[slot, 1].astype(BF16))
        md_scr[...], ld_scr[...], accd_scr[...] = st
        s = _dot_nt(qf, kv_buf[slot, 2].astype(BF16)) - ck_rows + tbl_f[...]
        st = _online_update((mf_scr[...], lf_scr[...], accf_scr[...]), s, 0.0, kv_buf[slot, 3].astype(BF16))
        mf_scr[...], lf_scr[...], accf_scr[...] = st
        return 0

    lax.fori_loop(0, n_chunks, chunk_body, 0)

    new_kv[...] = jnp.zeros(new_kv.shape, BF16)
    for ti, grp in enumerate(KV_GROUPS):
        new_kv[ti, 0:rq, :] = zs_ref[grp]
    lam = _lambda_full(lq1_ref, lk1_ref, lq2_ref, lk2_ref, lam_init)
    def new_token_mask(n_rows):
        rr = lax.broadcasted_iota(jnp.int32, (n_rows, LANE), 0)
        cc = lax.broadcasted_iota(jnp.int32, (n_rows, LANE), 1)
        return ((rr % N_HEADS) == (cc % N_HEADS)) & (cc // N_HEADS <= (rr % rq) // N_HEADS) & (cc < rq)

    s = _dot_nt(qd, new_kv[0]) + bn_ref[...]
    s = jnp.where(new_token_mask(2 * rq), s, NEG_BIG)
    _, l, acc = _online_update((md_scr[...], ld_scr[...], accd_scr[...]), s, 0.0, new_kv[1])
    y = _diff_finish(acc, l, rq, lam, lam_init, subln_ref[...]) * zs_ref[G_DG].astype(F32)
    yd_ref[...] = y.astype(BF16)

    lfn = lfn_ref[...]
    cnew = [carry_scr[...] + lfn[:, 0:1]]
    for j in range(1, t_new):
        cnew.append(cnew[-1] + lfn[:, j:j + 1])
    lane_n = lax.broadcasted_iota(jnp.int32, (N_HEADS, LANE), 1)
    ck_new = jnp.zeros((N_HEADS, LANE), F32)
    for j in range(t_new):
        ck_new = jnp.where(lane_n // N_HEADS == j, cnew[j], ck_new)
    ck_new_rows = jnp.concatenate([ck_new] * t_new, axis=0)
    cq = jnp.concatenate(cnew, axis=0)
    s = _dot_nt(qf, new_kv[2]) - ck_new_rows
    s = jnp.where(new_token_mask(rq), s, NEG_BIG)
    m_all = mf_scr[...] + cq
    m_new = jnp.maximum(m_all, jnp.max(s, axis=-1, keepdims=True) + cq)
    p = jnp.exp(s - (m_new - cq))
    alpha = jnp.exp(m_all - m_new)
    l_fin = alpha * lf_scr[...] + jnp.sum(p, axis=-1, keepdims=True)
    acc_fin = alpha * accf_scr[...] + _dot(p.astype(BF16), new_kv[3])
    yf_ref[...] = (acc_fin / l_fin * zs_ref[G_FG].astype(F32)).astype(BF16)


def _dec_call(page_table, rel_bias, zs, lfn, bias_last, bias_new, lq1, lk1, lq2, lk2, subln,
              cdk, cdv, cfk, cfv, clf, l, t_new):
    _, nb, rq, _ = zs.shape
    n_pages = page_table.shape[1]
    page = clf.shape[3]
    prow = page * N_HEADS
    n_chunks = n_pages // DEC_CHUNK_PAGES
    tcf = DEC_CHUNK_PAGES * prow

    def pspec(n):
        return pl.BlockSpec((None, 1, n), lambda b, pt: (l, 0, 0))

    any_spec = pl.BlockSpec(memory_space=pl.ANY)
    y_spec = pl.BlockSpec((None, rq, HEAD_DIM), lambda b, pt: (b, 0, 0))
    y_shape = jax.ShapeDtypeStruct((nb, rq, HEAD_DIM), BF16)
    grid_spec = pltpu.PrefetchScalarGridSpec(
        num_scalar_prefetch=1,
        grid=(nb,),
        in_specs=[
            pl.BlockSpec(memory_space=pltpu.SMEM),
            pl.BlockSpec((8, None, rq, HEAD_DIM), lambda b, pt: (0, b, 0, 0)),
            pl.BlockSpec((None, N_HEADS, t_new), lambda b, pt: (b, 0, 0)),
            pl.BlockSpec((2 * rq, prow), lambda b, pt: (0, 0)),
            pl.BlockSpec((2 * rq, LANE), lambda b, pt: (0, 0)),
            pspec(DIFF_QK_DIM), pspec(DIFF_QK_DIM), pspec(DIFF_QK_DIM), pspec(DIFF_QK_DIM),
            pspec(HEAD_DIM),
            any_spec, any_spec, any_spec, any_spec, any_spec,
        ],
        out_specs=[y_spec, y_spec],
        scratch_shapes=[
            pltpu.VMEM((DEC_SLOTS, 4, tcf, HEAD_DIM), F32),
            pltpu.VMEM((DEC_SLOTS, DEC_CHUNK_PAGES, N_HEADS, page), F32),
            pltpu.SemaphoreType.DMA((DEC_SLOTS,)),
            pltpu.VMEM((4, LANE, HEAD_DIM), BF16),
            pltpu.VMEM((2, 2 * rq, tcf), F32),
            pltpu.VMEM((rq, tcf), F32),
            pltpu.VMEM((2 * rq, 1), F32),
            pltpu.VMEM((2 * rq, 1), F32), pltpu.VMEM((2 * rq, 1), F32), pltpu.VMEM((2 * rq, HEAD_DIM), F32),
            pltpu.VMEM((rq, 1), F32), pltpu.VMEM((rq, 1), F32), pltpu.VMEM((rq, HEAD_DIM), F32),
            pltpu.VMEM((N_HEADS, 1), F32),
        ],
    )
    return pl.pallas_call(
        functools.partial(_dec_kernel, layer=l, lam_init=_lambda_init(l), n_chunks=n_chunks, n_batch=nb, t_new=t_new),
        grid_spec=grid_spec,
        out_shape=[y_shape, y_shape],
        compiler_params=_cparams("arbitrary"),
        name="sample_paged_attention",
    )(page_table, rel_bias, zs, lfn, bias_last, bias_new, lq1, lk1, lq2, lk2, subln, cdk, cdv, cfk, cfv, clf)


def kernel(x_prompt, x_sample, c_prompt, c_sample, cache_diff_k, cache_diff_v, cache_fox_k, cache_fox_v, cache_fox_logf, page_table, w_ada, b_ada, norm_pre, norm_post, w_in, b_forget, lambda_q1, lambda_k1, lambda_q2, lambda_k2, diff_subln, rel_bias, w_out):
    batch, seq, d = x_prompt.shape
    nb, t_new, _ = x_sample.shape
    depth = w_ada.shape[0]
    n_pool, page = cache_diff_k.shape[1], cache_diff_k.shape[2]
    width = N_HEADS * HEAD_DIM
    blk = ATTN_BLOCK
    m_p, m_s = batch * seq, nb * t_new
    rq = t_new * N_HEADS
    prow = page * N_HEADS

    far = _far_distance()
    assert far <= blk + 1 and far <= page + 1 and page == LANE and rq <= LANE

    w_in_bf = w_in.astype(BF16)
    wf = w_in[:, :, 8 * width:]
    wf_pad = jnp.pad(wf, ((0, 0), (0, 0), (0, LANE - N_HEADS))).astype(BF16)
    wft_pad = jnp.pad(jnp.swapaxes(wf, 1, 2), ((0, 0), (0, 2 * SUBLANE - N_HEADS), (0, 0))).astype(BF16)
    w_out_bf = w_out.astype(BF16)
    bf_row = b_forget.reshape(depth, 1, N_HEADS)
    bf_col = b_forget.reshape(depth, N_HEADS, 1)
    g_pre = norm_pre.reshape(depth, 1, d)
    g_post = norm_post.reshape(depth, 1, d)
    lq1 = lambda_q1.reshape(depth, 1, DIFF_QK_DIM)
    lk1 = lambda_k1.reshape(depth, 1, DIFF_QK_DIM)
    lq2 = lambda_q2.reshape(depth, 1, DIFF_QK_DIM)
    lk2 = lambda_k2.reshape(depth, 1, DIFF_QK_DIM)
    subln = diff_subln.reshape(depth, 1, HEAD_DIM)

    rows = -(-(batch + nb) // (2 * SUBLANE)) * (2 * SUBLANE)
    c_all = jnp.concatenate([c_prompt, c_sample, jnp.zeros((rows - batch - nb, d), F32)], axis=0)
    mod = _ada_call(c_all, w_ada, b_ada)

    r_i = jnp.arange(blk, dtype=jnp.int32)
    dist = r_i[:, None] - r_i[None, :]
    idx_p = jnp.concatenate([_t5_bucket(jnp.maximum(dist, 0)), _t5_bucket(dist + blk)], axis=0)
    bias_p = _bias_call(rel_bias, idx_p, LOG2E).reshape(N_HEADS, 2, blk, blk)
    tq = jnp.arange(2 * t_new, dtype=jnp.int32)[:, None] % t_new
    pos_last = jnp.arange(prow, dtype=jnp.int32)[None, :] // N_HEADS
    pos_new = jnp.arange(LANE, dtype=jnp.int32)[None, :] // N_HEADS
    idx_dec = jnp.concatenate([_t5_bucket(tq + page - pos_last), _t5_bucket(jnp.maximum(tq - pos_new, 0))], axis=1)
    bias_dec = _bias_call(rel_bias, idx_dec)
    bias_dec = jnp.transpose(bias_dec.reshape(N_HEADS, 2, t_new, prow + LANE), (1, 2, 0, 3)).reshape(2 * rq, prow + LANE)
    bias_last, bias_new = bias_dec[:, :prow], bias_dec[:, prow:]

    cdk = cache_diff_k.reshape(depth, n_pool, prow, HEAD_DIM)
    cdv = cache_diff_v.reshape(depth, n_pool, prow, HEAD_DIM)
    cfk = cache_fox_k.reshape(depth, n_pool, prow, HEAD_DIM)
    cfv = cache_fox_v.reshape(depth, n_pool, prow, HEAD_DIM)
    clf = jnp.swapaxes(cache_fox_logf, 2, 3)

    xp = x_prompt.reshape(m_p, d)
    xs = x_sample.reshape(m_s, d)
    tm_p = min(1024, seq)
    kv_p = [jnp.zeros((depth, m_p, N_HEADS, HEAD_DIM), F32) for _ in KV_GROUPS]
    kv_s = [jnp.zeros((depth, m_s, N_HEADS, HEAD_DIM), F32) for _ in KV_GROUPS]
    logf_p, logf_s = [], []
    for l in range(depth):
        mp = mod[l, :batch]
        ms = jnp.repeat(mod[l, batch:batch + nb], t_new, axis=0)
        shift_p, scale_p, gate_p = (mp[:, None, k * d:(k + 1) * d] for k in range(3))
        shift_s, scale_s, gate_s = (ms[None, :, k * d:(k + 1) * d] for k in range(3))

        zb, logf, logft, kv_p = _proj_call(xp, shift_p, scale_p, g_pre, w_in_bf, wf_pad, wft_pad, bf_row, bf_col,
                                           kv_p, l, tm_p, LOG2E, "prompt_in_proj")
        ck_rows, cum_cols = _cum_call(logft, batch)
        yd, yf = _attn_call(zb, bias_p, ck_rows, cum_cols, rel_bias, lq1, lk1, lq2, lk2, subln, l, batch, blk)
        xp = _out_call(yd, yf, w_out_bf, xp, gate_p, g_post, l, min(512, seq), "prompt_out_proj")
        logf_p.append(logf.reshape(batch, seq, N_HEADS))

        zs, logf_n, logft_n, kv_s = _proj_call(xs, shift_s, scale_s, g_pre, w_in_bf, wf_pad, wft_pad, bf_row, bf_col,
                                               kv_s, l, m_s, 1.0, "sample_in_proj")
        lfn = jnp.transpose(logft_n.reshape(N_HEADS, nb, t_new), (1, 0, 2))
        yds, yfs = _dec_call(page_table, rel_bias, zs.reshape(8, nb, rq, HEAD_DIM), lfn, bias_last, bias_new,
                             lq1, lk1, lq2, lk2, subln, cdk, cdv, cfk, cfv, clf, l, t_new)
        xs = _out_call(yds.reshape(m_s, width), yfs.reshape(m_s, width), w_out_bf, xs, gate_s, g_post,
                       l, m_s, "sample_out_proj")
        logf_s.append(logf_n.reshape(nb, t_new, N_HEADS))

    outs_p = [a.reshape(depth, batch, seq, N_HEADS, HEAD_DIM) for a in kv_p]
    outs_s = [a.reshape(depth, nb, t_new, N_HEADS, HEAD_DIM) for a in kv_s]
    return (xp.reshape(batch, seq, d), xs.reshape(nb, t_new, d), *outs_p, jnp.stack(logf_p, axis=0),
            *outs_s, jnp.stack(logf_s, axis=0))
```

```python
import functools
import math

import numpy as np
import jax
import jax.numpy as jnp
from jax import lax
from jax.experimental import pallas as pl
from jax.experimental.pallas import tpu as pltpu

F32 = jnp.float32
BF16 = jnp.bfloat16

EPS = 1e-6
HEAD_DIM = 128
DIFF_QK_DIM = 64
N_HEADS = 8
N_BUCKETS = 32
MAX_DISTANCE = 128
NEG_BIG = -1e30
LOG2E = math.log2(math.e)

V7X_VMEM_LIMIT_BYTES = 56 * 1024 * 1024
LANE = 128
SUBLANE = 8
MXU_COLS = 256

ATTN_BLOCK = 512
DEC_CHUNK_PAGES = 4
DEC_SLOTS = 3

G_DQ, G_DK, G_DV, G_DG, G_FQ, G_FK, G_FV, G_FG = range(8)
KV_GROUPS = (G_DK, G_DV, G_FK, G_FV)


def _cparams(*sem):
    return pltpu.CompilerParams(dimension_semantics=sem, vmem_limit_bytes=V7X_VMEM_LIMIT_BYTES)


def _lambda_init(l):
    return 0.8 - 0.6 * math.exp(-0.3 * l)


def _log_sigmoid(x):
    return jnp.minimum(x, 0.0) - jnp.log1p(jnp.exp(-jnp.abs(x)))


def _silu(x):
    return x * jax.nn.sigmoid(x)


def _dot(a, b):
    return jnp.dot(a, b, preferred_element_type=F32)


def _dot_nt(a, b):
    return lax.dot_general(a, b, (((1,), (1,)), ((), ())), preferred_element_type=F32)


def _ada_kernel(c_ref, w_ref, b_ref, o_ref):
    sc = _silu(c_ref[...]).astype(BF16)
    o_ref[...] = _dot(sc, w_ref[...].astype(BF16)) + b_ref[...]


def _ada_call(c_all, w_ada, b_ada, tn=768):
    depth, d, e = w_ada.shape
    rows = c_all.shape[0]
    return pl.pallas_call(
        _ada_kernel,
        grid=(depth, e // tn),
        in_specs=[
            pl.BlockSpec((rows, d), lambda l, j: (0, 0)),
            pl.BlockSpec((None, d, tn), lambda l, j: (l, 0, j)),
            pl.BlockSpec((None, 1, tn), lambda l, j: (l, 0, j)),
        ],
        out_specs=pl.BlockSpec((None, rows, tn), lambda l, j: (l, 0, j)),
        out_shape=jax.ShapeDtypeStruct((depth, rows, e), F32),
        compiler_params=_cparams("arbitrary", "arbitrary"),
        name="ada_mod",
    )(c_all, w_ada, b_ada.reshape(depth, 1, e))


def _bias_kernel(rel_ref, idx_ref, o_ref, *, scale):
    h = pl.program_id(0)
    idx = idx_ref[...]
    acc = jnp.zeros(idx.shape, F32)
    for b in range(N_BUCKETS):
        acc = jnp.where(idx == b, rel_ref[b, h], acc)
    o_ref[...] = acc * scale


def _bias_call(rel_bias, idx, scale=1.0):
    r, c = idx.shape
    return pl.pallas_call(
        functools.partial(_bias_kernel, scale=scale),
        grid=(N_HEADS,),
        in_specs=[
            pl.BlockSpec(memory_space=pltpu.SMEM),
            pl.BlockSpec((r, c), lambda h: (0, 0)),
        ],
        out_specs=pl.BlockSpec((None, r, c), lambda h: (h, 0, 0)),
        out_shape=jax.ShapeDtypeStruct((N_HEADS, r, c), F32),
        compiler_params=_cparams("arbitrary"),
        name="rel_bias_tables",
    )(rel_bias, idx)


def _t5_bucket(n):
    max_exact = N_BUCKETS // 2
    large = max_exact + (jnp.log(jnp.maximum(n, 1).astype(F32) / max_exact)
                         / math.log(MAX_DISTANCE / max_exact) * (N_BUCKETS - max_exact)).astype(jnp.int32)
    large = jnp.minimum(large, N_BUCKETS - 1)
    return jnp.where(n < max_exact, n, large)


def _far_distance():
    n = np.arange(1, 4 * MAX_DISTANCE)
    large = 16 + (np.log(n.astype(np.float32) / 16) / math.log(MAX_DISTANCE / 16) * 16).astype(np.int32)
    bucket = np.where(n < 16, n, np.minimum(large, N_BUCKETS - 1))
    not_last = np.nonzero(bucket != N_BUCKETS - 1)[0]
    return int(n[not_last[-1]]) + 1


def _proj_kernel(*refs, layer, q_scales):
    n_kv = len(KV_GROUPS)
    x_ref, shift_ref, scale_ref, g_ref, w_ref, wf_ref, wft_ref, bfr_ref, bfc_ref = refs[:9]
    zb_ref, lf_ref, lft_ref = refs[9 + n_kv:12 + n_kv]
    kv_refs = refs[12 + n_kv:12 + 2 * n_kv]
    h_scr, stage, sem = refs[12 + 2 * n_kv:]
    i = pl.program_id(0)
    j = pl.program_id(1)
    tm = x_ref.shape[0]
    n_col = zb_ref.shape[1] // MXU_COLS
    n_heads = zb_ref.shape[1] // HEAD_DIM

    @pl.when(j == 0)
    def _():
        x = x_ref[...]
        ms = jnp.mean(x * x, axis=-1, keepdims=True)
        y = x * lax.rsqrt(ms + EPS) * g_ref[...]
        h = (y * (1.0 + scale_ref[...]) + shift_ref[...]).astype(BF16)
        h_scr[...] = h
        ff = _dot(h, wf_ref[...])[:, :N_HEADS] + bfr_ref[...]
        lf_ref[...] = _log_sigmoid(ff)
        fft = _dot_nt(wft_ref[...], h)[:N_HEADS, :] + bfc_ref[...]
        lft_ref[...] = _log_sigmoid(fft)

    def kv_copies(t, row0):
        return [pltpu.make_async_copy(stage.at[t % 2, :, pl.ds(hh * HEAD_DIM, HEAD_DIM)],
                                      kv_refs[t].at[layer, pl.ds(row0, tm), hh, :], sem.at[t % 2])
                for hh in range(n_heads)]

    def emit(transform, kv_t=None):
        for c in range(n_col):
            cs = slice(c * MXU_COLS, (c + 1) * MXU_COLS)
            z = _dot(h_scr[...], w_ref[:, cs])
            zb_ref[:, cs] = transform(z).astype(BF16)
            if kv_t is not None:
                stage[kv_t % 2, :, cs] = z

    def plain(z):
        return z

    @pl.when(j == G_DQ)
    def _():
        emit(lambda z: z * q_scales[0])

    @pl.when(j == G_FQ)
    def _():
        emit(lambda z: z * q_scales[1])

    @pl.when((j == G_DG) | (j == G_FG))
    def _():
        emit(_silu)

    for t, grp in enumerate(KV_GROUPS):
        @pl.when(j == grp)
        def _(t=t):
            if t >= 2:
                for cp in kv_copies(t - 2, 0):
                    cp.wait()
            else:
                @pl.when(i > 0)
                def _():
                    for cp in kv_copies(t + 2, 0):
                        cp.wait()
            emit(plain, t)
            for cp in kv_copies(t, i * tm):
                cp.start()

    @pl.when((i == pl.num_programs(0) - 1) & (j == pl.num_programs(1) - 1))
    def _():
        for cp in kv_copies(2, 0) + kv_copies(3, 0):
            cp.wait()


def _proj_call(x, shift, scale, g_pre, w_in_bf, wf_pad, wft_pad, bf_row, bf_col, kv_stacks, l, tm, logit_scale, name):
    m, d = x.shape
    groups, r, _ = shift.shape
    tiles_per_group = (m // tm) // groups
    width = (w_in_bf.shape[2] - N_HEADS) // 8
    nh = width // HEAD_DIM
    mod_spec = pl.BlockSpec((None, r, d), lambda i, j: (i // tiles_per_group, 0, 0))
    any_spec = pl.BlockSpec(memory_space=pl.ANY)
    n_kv = len(KV_GROUPS)
    kv_shape = jax.ShapeDtypeStruct(kv_stacks[0].shape, F32)
    q_scales = (DIFF_QK_DIM ** -0.5 * logit_scale, HEAD_DIM ** -0.5 * logit_scale)
    outs = pl.pallas_call(
        functools.partial(_proj_kernel, layer=l, q_scales=q_scales),
        grid=(m // tm, 8),
        in_specs=[
            pl.BlockSpec((tm, d), lambda i, j: (i, 0)),
            mod_spec, mod_spec,
            pl.BlockSpec((None, 1, d), lambda i, j: (l, 0, 0)),
            pl.BlockSpec((None, d, width), lambda i, j: (l, 0, j)),
            pl.BlockSpec((None, d, LANE), lambda i, j: (l, 0, 0)),
            pl.BlockSpec((None, 2 * SUBLANE, d), lambda i, j: (l, 0, 0)),
            pl.BlockSpec((None, 1, N_HEADS), lambda i, j: (l, 0, 0)),
            pl.BlockSpec((None, N_HEADS, 1), lambda i, j: (l, 0, 0)),
        ] + [any_spec] * n_kv,
        out_specs=[
            pl.BlockSpec((None, tm, width), lambda i, j: (j, i, 0)),
            pl.BlockSpec((tm, N_HEADS), lambda i, j: (i, 0)),
            pl.BlockSpec((N_HEADS, tm), lambda i, j: (0, i)),
        ] + [any_spec] * n_kv,
        out_shape=[
            jax.ShapeDtypeStruct((8, m, width), BF16),
            jax.ShapeDtypeStruct((m, N_HEADS), F32),
            jax.ShapeDtypeStruct((N_HEADS, m), F32),
        ] + [kv_shape] * n_kv,
        scratch_shapes=[
            pltpu.VMEM((tm, d), BF16),
            pltpu.VMEM((2, tm, width), F32),
            pltpu.SemaphoreType.DMA((2,)),
        ],
        input_output_aliases={9 + t: 3 + t for t in range(n_kv)},
        compiler_params=_cparams("arbitrary", "arbitrary"),
        name=name,
    )(x, shift, scale, g_pre, w_in_bf, wf_pad, wft_pad, bf_row, bf_col, *kv_stacks)
    return outs[0], outs[1], outs[2], list(outs[3:])


def _cum_kernel(x_ref, row_ref, col_ref):
    x = x_ref[...]
    t = x.shape[1]
    lane = lax.broadcasted_iota(jnp.int32, x.shape, 1)
    s = 1
    while s < t:
        x = x + jnp.where(lane >= s, pltpu.roll(x, s, 1), 0.0)
        s *= 2
    x = x * LOG2E
    row_ref[...] = x
    xp = jnp.concatenate([x, jnp.zeros((LANE - x.shape[0], t), F32)], axis=0)
    col_ref[...] = xp.T


def _cum_call(logft, batch):
    nh, m = logft.shape
    t = m // batch
    return pl.pallas_call(
        _cum_kernel,
        grid=(batch,),
        in_specs=[pl.BlockSpec((nh, t), lambda b: (0, b))],
        out_specs=[
            pl.BlockSpec((None, nh, t), lambda b: (b, 0, 0)),
            pl.BlockSpec((None, t, LANE), lambda b: (b, 0, 0)),
        ],
        out_shape=[
            jax.ShapeDtypeStruct((batch, nh, t), F32),
            jax.ShapeDtypeStruct((batch, t, LANE), F32),
        ],
        compiler_params=_cparams("arbitrary"),
        name="prompt_logf_cumsum",
    )(logft)


def _lambda_full(lq1_ref, lk1_ref, lq2_ref, lk2_ref, lam_init):
    a = jnp.sum(lq1_ref[...] * lk1_ref[...], axis=-1, keepdims=True)
    b = jnp.sum(lq2_ref[...] * lk2_ref[...], axis=-1, keepdims=True)
    return jnp.exp(a) - jnp.exp(b) + lam_init


def _diff_rows(q):
    lane = lax.broadcasted_iota(jnp.int32, q.shape, 1)
    zero = jnp.zeros_like(q)
    return jnp.concatenate([jnp.where(lane < DIFF_QK_DIM, q, zero),
                            jnp.where(lane >= DIFF_QK_DIM, q, zero)], axis=0)


def _online_update(state, logits, shift_back, v, exp=jnp.exp):
    m, l, acc = state
    m_new = jnp.maximum(m, jnp.max(logits, axis=-1, keepdims=True) + shift_back)
    p = exp(logits - (m_new - shift_back))
    alpha = exp(m - m_new)
    l = alpha * l + jnp.sum(p, axis=-1, keepdims=True)
    acc = alpha * acc + _dot(p.astype(BF16), v)
    return m_new, l, acc


def _init_state(rows):
    return (jnp.full((rows, 1), NEG_BIG, F32), jnp.zeros((rows, 1), F32), jnp.zeros((rows, HEAD_DIM), F32))


def _diff_finish(acc, l, rows, lam, lam_init, subln):
    o = acc[:rows] / l[:rows] - lam * (acc[rows:] / l[rows:])
    return o * lax.rsqrt(jnp.mean(o * o, axis=-1, keepdims=True) + EPS) * subln * (1.0 - lam_init)


def _attn_kernel(rel_ref, dq_ref, dk_ref, dv_ref, dg_ref, fq_ref, fk_ref, fv_ref, fg_ref, bias_ref, ck_ref, cc_ref,
                 lq1_ref, lk1_ref, lq2_ref, lk2_ref, subln_ref, yd_ref, yf_ref, *, lam_init, blk):
    t = dq_ref.shape[0]
    nq = t // blk
    h = pl.program_id(1)
    c_far = rel_ref[N_BUCKETS - 1, h] * LOG2E
    lam = _lambda_full(lq1_ref, lk1_ref, lq2_ref, lk2_ref, lam_init)
    row = lax.broadcasted_iota(jnp.int32, (blk, blk), 0)
    col = lax.broadcasted_iota(jnp.int32, (blk, blk), 1)
    causal = col <= row
    bias_d = jnp.where(causal, bias_ref[0], NEG_BIG)
    mask_d = jnp.where(causal, 0.0, NEG_BIG)
    bias_nd = jnp.concatenate([bias_ref[1], bias_d], axis=1)
    mask_nd = jnp.concatenate([jnp.zeros((blk, blk), F32), mask_d], axis=1)
    bias_d2 = jnp.concatenate([bias_d, bias_d], axis=0)
    bias_nd2 = jnp.concatenate([bias_nd, bias_nd], axis=0)
    head_lane = lax.broadcasted_iota(jnp.int32, (blk, LANE), 1) == h
    upd = functools.partial(_online_update, exp=jnp.exp2)

    for qi in range(nq):
        qs = slice(qi * blk, (qi + 1) * blk)
        qd = _diff_rows(dq_ref[qs, :])
        qf = fq_ref[qs, :]
        cq = jnp.sum(jnp.where(head_lane, cc_ref[qs, :], 0.0), axis=-1, keepdims=True)
        sd, sf = _init_state(2 * blk), _init_state(blk)
        far_w = max(qi - 1, 0) * blk
        if far_w:
            s = _dot_nt(qd, dk_ref[0:far_w, :])
            sd = upd(sd, s, c_far, dv_ref[0:far_w, :])
            s = _dot_nt(qf, fk_ref[0:far_w, :]) - ck_ref[:, 0:far_w]
            sf = upd(sf, s, cq, fv_ref[0:far_w, :])
        ks = far_w
        ke = (qi + 1) * blk
        s = _dot_nt(qd, dk_ref[ks:ke, :]) + (bias_nd2 if qi else bias_d2)
        _, l, acc = upd(sd, s, 0.0, dv_ref[ks:ke, :])
        y = _diff_finish(acc, l, blk, lam, lam_init, subln_ref[...]) * dg_ref[qs, :].astype(F32)
        yd_ref[qs, :] = y.astype(BF16)
        s = _dot_nt(qf, fk_ref[ks:ke, :]) - ck_ref[:, ks:ke] + (mask_nd if qi else mask_d)
        _, l, acc = upd(sf, s, cq, fv_ref[ks:ke, :])
        yf_ref[qs, :] = (acc / l * fg_ref[qs, :].astype(F32)).astype(BF16)


def _attn_call(zb, bias_p, ck_rows, cum_cols, rel_bias, lq1, lk1, lq2, lk2, subln, l, batch, blk):
    _, m, width = zb.shape
    t = m // batch
    nh = width // HEAD_DIM

    def zspec(k):
        return pl.BlockSpec((None, t, HEAD_DIM), lambda b, h: (k, b, h))

    def pspec(n):
        return pl.BlockSpec((None, 1, n), lambda b, h: (l, 0, 0))

    y_spec = pl.BlockSpec((t, HEAD_DIM), lambda b, h: (b, h))
    y_shape = jax.ShapeDtypeStruct((m, width), BF16)
    return pl.pallas_call(
        functools.partial(_attn_kernel, lam_init=_lambda_init(l), blk=blk),
        grid=(batch, nh),
        in_specs=[
            pl.BlockSpec(memory_space=pltpu.SMEM),
            zspec(G_DQ), zspec(G_DK), zspec(G_DV), zspec(G_DG), zspec(G_FQ), zspec(G_FK), zspec(G_FV), zspec(G_FG),
            pl.BlockSpec((None, 2, blk, blk), lambda b, h: (h, 0, 0, 0)),
            pl.BlockSpec((None, 1, t), lambda b, h: (b * nh + h, 0, 0)),
            pl.BlockSpec((None, t, LANE), lambda b, h: (b, 0, 0)),
            pspec(DIFF_QK_DIM), pspec(DIFF_QK_DIM), pspec(DIFF_QK_DIM), pspec(DIFF_QK_DIM),
            pspec(HEAD_DIM),
        ],
        out_specs=[y_spec, y_spec],
        out_shape=[y_shape, y_shape],
        compiler_params=_cparams("arbitrary", "arbitrary"),
        name="prompt_attention",
    )(rel_bias, zb, zb, zb, zb, zb, zb, zb, zb, bias_p, ck_rows.reshape(batch * nh, 1, t), cum_cols,
      lq1, lk1, lq2, lk2, subln)


def _out_kernel(yd_ref, yf_ref, w_ref, x_ref, gate_ref, g_ref, o_ref):
    half = yd_ref.shape[1]
    y = _dot(yd_ref[...], w_ref[:half, :]) + _dot(yf_ref[...], w_ref[half:, :])
    r = y * lax.rsqrt(jnp.mean(y * y, axis=-1, keepdims=True) + EPS) * g_ref[...]
    o_ref[...] = x_ref[...] + gate_ref[...] * r


def _out_call(yd, yf, w_out_bf, x, gate, g_post, l, tm, name):
    m, d = x.shape
    half = yd.shape[1]
    groups, r, _ = gate.shape
    tiles_per_group = (m // tm) // groups
    return pl.pallas_call(
        _out_kernel,
        grid=(m // tm,),
        in_specs=[
            pl.BlockSpec((tm, half), lambda i: (i, 0)),
            pl.BlockSpec((tm, half), lambda i: (i, 0)),
            pl.BlockSpec((None, 2 * half, d), lambda i: (l, 0, 0)),
            pl.BlockSpec((tm, d), lambda i: (i, 0)),
            pl.BlockSpec((None, r, d), lambda i: (i // tiles_per_group, 0, 0)),
            pl.BlockSpec((None, 1, d), lambda i: (l, 0, 0)),
        ],
        out_specs=pl.BlockSpec((tm, d), lambda i: (i, 0)),
        out_shape=jax.ShapeDtypeStruct((m, d), F32),
        compiler_params=_cparams("arbitrary"),
        name=name,
    )(yd, yf, w_out_bf, x, gate, g_post)


def _dec_kernel(pt_ref, rel_ref, zs_ref, lfn_ref, bl_ref, bn_ref, lq1_ref, lk1_ref, lq2_ref, lk2_ref, subln_ref,
                cdk_ref, cdv_ref, cfk_ref, cfv_ref, clf_ref,
                yd_ref, yf_ref,
                kv_buf, lf_buf, sem, new_kv, tbl_d, tbl_f, cfar_scr,
                md_scr, ld_scr, accd_scr, mf_scr, lf_scr, accf_scr, carry_scr,
                *, layer, lam_init, n_chunks, n_batch, t_new):
    b = pl.program_id(0)
    cp = DEC_CHUNK_PAGES
    page = lf_buf.shape[3]
    prow = page * N_HEADS
    tcf = cp * prow
    rq = t_new * N_HEADS
    total = n_batch * n_chunks
    caches = (cdk_ref, cdv_ref, cfk_ref, cfv_ref)

    def chunk_copies(g, slot):
        gb = g // n_chunks
        gc = g % n_chunks
        out = []
        for j in range(cp):
            pg = pt_ref[gb, gc * cp + j]
            for ti, c_ref in enumerate(caches):
                out.append(pltpu.make_async_copy(c_ref.at[layer, pg], kv_buf.at[slot, ti, pl.ds(j * prow, prow), :],
                                                 sem.at[slot]))
            out.append(pltpu.make_async_copy(clf_ref.at[layer, pg], lf_buf.at[slot, j], sem.at[slot]))
        return out

    @pl.when(b == 0)
    def _():
        for g in range(DEC_SLOTS - 1):
            for c in chunk_copies(g, g):
                c.start()
        rr = lax.broadcasted_iota(jnp.int32, (2 * rq, tcf), 0) % N_HEADS
        cc = lax.broadcasted_iota(jnp.int32, (2 * rq, tcf), 1) % N_HEADS
        base = jnp.where(rr == cc, 0.0, NEG_BIG)
        row_head = lax.broadcasted_iota(jnp.int32, (2 * rq, 1), 0) % N_HEADS
        cfar = jnp.zeros((2 * rq, 1), F32)
        for h in range(N_HEADS):
            cfar = jnp.where(row_head == h, rel_ref[N_BUCKETS - 1, h], cfar)
        cfar_scr[...] = cfar
        tbl_d[0] = base
        tbl_d[1, :, :tcf - prow] = base[:, :tcf - prow]
        rr_p = lax.broadcasted_iota(jnp.int32, (2 * rq, prow), 0) % N_HEADS
        cc_p = lax.broadcasted_iota(jnp.int32, (2 * rq, prow), 1) % N_HEADS
        tbl_d[1, :, tcf - prow:] = jnp.where(rr_p == cc_p, bl_ref[...] - cfar_scr[...], NEG_BIG)
        tbl_f[...] = base[:rq]

    md_scr[...] = jnp.full(md_scr.shape, NEG_BIG, F32)
    ld_scr[...] = jnp.zeros(ld_scr.shape, F32)
    accd_scr[...] = jnp.zeros(accd_scr.shape, F32)
    mf_scr[...] = jnp.full(mf_scr.shape, NEG_BIG, F32)
    lf_scr[...] = jnp.zeros(lf_scr.shape, F32)
    accf_scr[...] = jnp.zeros(accf_scr.shape, F32)
    carry_scr[...] = jnp.zeros(carry_scr.shape, F32)

    qd = _diff_rows(zs_ref[G_DQ])
    qf = zs_ref[G_FQ]
    tri_r = lax.broadcasted_iota(jnp.int32, (page, page), 0)
    tri_c = lax.broadcasted_iota(jnp.int32, (page, page), 1)
    upper = (tri_r <= tri_c).astype(BF16)
    lane = lax.broadcasted_iota(jnp.int32, (N_HEADS, page), 1)
    spread = [(page // N_HEADS) * k + lane // N_HEADS for k in range(N_HEADS)]

    def chunk_body(c, _):
        g = b * n_chunks + c
        slot = g % DEC_SLOTS
        nxt = g + DEC_SLOTS - 1

        @pl.when(nxt < total)
        def _():
            for cpy in chunk_copies(nxt, nxt % DEC_SLOTS):
                cpy.start()

        for cpy in chunk_copies(g, slot):
            cpy.wait()

        local = []
        for j in range(cp):
            x = lf_buf[slot, j]
            hi = x.astype(BF16)
            r1 = x - hi.astype(F32)
            mid = r1.astype(BF16)
            lo = (r1 - mid.astype(F32)).astype(BF16)
            local.append(_dot(hi, upper) + _dot(mid, upper) + _dot(lo, upper))
        carry = carry_scr[...]
        parts = []
        for j in range(cp):
            ck = local[j] + carry
            carry = ck[:, page - 1:page]
            parts.extend(jnp.take_along_axis(ck, idx, axis=1) for idx in spread)
        carry_scr[...] = carry
        ck_cols = jnp.concatenate(parts, axis=1)
        ck_rows = jnp.concatenate([ck_cols] * t_new, axis=0)

        last = jnp.where(c == n_chunks - 1, 1, 0)
        s = _dot_nt(qd, kv_buf[slot, 0].astype(BF16)) + tbl_d[last]
        st = _online_update((md_scr[...], ld_scr[...], accd_scr[...]), s, cfar_scr[...], kv_buf[slot, 1].astype(BF16))
        md_scr[...], ld_scr[...], accd_scr[...] = st
        s = _dot_nt(qf, kv_buf[slot, 2].astype(BF16)) - ck_rows + tbl_f[...]
        st = _online_update((mf_scr[...], lf_scr[...], accf_scr[...]), s, 0.0, kv_buf[slot, 3].astype(BF16))
        mf_scr[...], lf_scr[...], accf_scr[...] = st
        return 0

    lax.fori_loop(0, n_chunks, chunk_body, 0)

    new_kv[...] = jnp.zeros(new_kv.shape, BF16)
    for ti, grp in enumerate(KV_GROUPS):
        new_kv[ti, 0:rq, :] = zs_ref[grp]
    lam = _lambda_full(lq1_ref, lk1_ref, lq2_ref, lk2_ref, lam_init)
    def new_token_mask(n_rows):
        rr = lax.broadcasted_iota(jnp.int32, (n_rows, LANE), 0)
        cc = lax.broadcasted_iota(jnp.int32, (n_rows, LANE), 1)
        return ((rr % N_HEADS) == (cc % N_HEADS)) & (cc // N_HEADS <= (rr % rq) // N_HEADS) & (cc < rq)

    s = _dot_nt(qd, new_kv[0]) + bn_ref[...]
    s = jnp.where(new_token_mask(2 * rq), s, NEG_BIG)
    _, l, acc = _online_update((md_scr[...], ld_scr[...], accd_scr[...]), s, 0.0, new_kv[1])
    y = _diff_finish(acc, l, rq, lam, lam_init, subln_ref[...]) * zs_ref[G_DG].astype(F32)
    yd_ref[...] = y.astype(BF16)

    lfn = lfn_ref[...]
    cnew = [carry_scr[...] + lfn[:, 0:1]]
    for j in range(1, t_new):
        cnew.append(cnew[-1] + lfn[:, j:j + 1])
    lane_n = lax.broadcasted_iota(jnp.int32, (N_HEADS, LANE), 1)
    ck_new = jnp.zeros((N_HEADS, LANE), F32)
    for j in range(t_new):
        ck_new = jnp.where(lane_n // N_HEADS == j, cnew[j], ck_new)
    ck_new_rows = jnp.concatenate([ck_new] * t_new, axis=0)
    cq = jnp.concatenate(cnew, axis=0)
    s = _dot_nt(qf, new_kv[2]) - ck_new_rows
    s = jnp.where(new_token_mask(rq), s, NEG_BIG)
    m_all = mf_scr[...] + cq
    m_new = jnp.maximum(m_all, jnp.max(s, axis=-1, keepdims=True) + cq)
    p = jnp.exp(s - (m_new - cq))
    alpha = jnp.exp(m_all - m_new)
    l_fin = alpha * lf_scr[...] + jnp.sum(p, axis=-1, keepdims=True)
    acc_fin = alpha * accf_scr[...] + _dot(p.astype(BF16), new_kv[3])
    yf_ref[...] = (acc_fin / l_fin * zs_ref[G_FG].astype(F32)).astype(BF16)


def _dec_call(page_table, rel_bias, zs, lfn, bias_last, bias_new, lq1, lk1, lq2, lk2, subln,
              cdk, cdv, cfk, cfv, clf, l, t_new):
    _, nb, rq, _ = zs.shape
    n_pages = page_table.shape[1]
    page = clf.shape[3]
    prow = page * N_HEADS
    n_chunks = n_pages // DEC_CHUNK_PAGES
    tcf = DEC_CHUNK_PAGES * prow

    def pspec(n):
        return pl.BlockSpec((None, 1, n), lambda b, pt: (l, 0, 0))

    any_spec = pl.BlockSpec(memory_space=pl.ANY)
    y_spec = pl.BlockSpec((None, rq, HEAD_DIM), lambda b, pt: (b, 0, 0))
    y_shape = jax.ShapeDtypeStruct((nb, rq, HEAD_DIM), BF16)
    grid_spec = pltpu.PrefetchScalarGridSpec(
        num_scalar_prefetch=1,
        grid=(nb,),
        in_specs=[
            pl.BlockSpec(memory_space=pltpu.SMEM),
            pl.BlockSpec((8, None, rq, HEAD_DIM), lambda b, pt: (0, b, 0, 0)),
            pl.BlockSpec((None, N_HEADS, t_new), lambda b, pt: (b, 0, 0)),
            pl.BlockSpec((2 * rq, prow), lambda b, pt: (0, 0)),
            pl.BlockSpec((2 * rq, LANE), lambda b, pt: (0, 0)),
            pspec(DIFF_QK_DIM), pspec(DIFF_QK_DIM), pspec(DIFF_QK_DIM), pspec(DIFF_QK_DIM),
            pspec(HEAD_DIM),
            any_spec, any_spec, any_spec, any_spec, any_spec,
        ],
        out_specs=[y_spec, y_spec],
        scratch_shapes=[
            pltpu.VMEM((DEC_SLOTS, 4, tcf, HEAD_DIM), F32),
            pltpu.VMEM((DEC_SLOTS, DEC_CHUNK_PAGES, N_HEADS, page), F32),
            pltpu.SemaphoreType.DMA((DEC_SLOTS,)),
            pltpu.VMEM((4, LANE, HEAD_DIM), BF16),
            pltpu.VMEM((2, 2 * rq, tcf), F32),
            pltpu.VMEM((rq, tcf), F32),
            pltpu.VMEM((2 * rq, 1), F32),
            pltpu.VMEM((2 * rq, 1), F32), pltpu.VMEM((2 * rq, 1), F32), pltpu.VMEM((2 * rq, HEAD_DIM), F32),
            pltpu.VMEM((rq, 1), F32), pltpu.VMEM((rq, 1), F32), pltpu.VMEM((rq, HEAD_DIM), F32),
            pltpu.VMEM((N_HEADS, 1), F32),
        ],
    )
    return pl.pallas_call(
        functools.partial(_dec_kernel, layer=l, lam_init=_lambda_init(l), n_chunks=n_chunks, n_batch=nb, t_new=t_new),
        grid_spec=grid_spec,
        out_shape=[y_shape, y_shape],
        compiler_params=_cparams("arbitrary"),
        name="sample_paged_attention",
    )(page_table, rel_bias, zs, lfn, bias_last, bias_new, lq1, lk1, lq2, lk2, subln, cdk, cdv, cfk, cfv, clf)


def kernel(x_prompt, x_sample, c_prompt, c_sample, cache_diff_k, cache_diff_v, cache_fox_k, cache_fox_v, cache_fox_logf, page_table, w_ada, b_ada, norm_pre, norm_post, w_in, b_forget, lambda_q1, lambda_k1, lambda_q2, lambda_k2, diff_subln, rel_bias, w_out):
    batch, seq, d = x_prompt.shape
    nb, t_new, _ = x_sample.shape
    depth = w_ada.shape[0]
    n_pool, page = cache_diff_k.shape[1], cache_diff_k.shape[2]
    width = N_HEADS * HEAD_DIM
    blk = ATTN_BLOCK
    m_p, m_s = batch * seq, nb * t_new
    rq = t_new * N_HEADS
    prow = page * N_HEADS

    far = _far_distance()
    assert far <= blk + 1 and far <= page + 1 and page == LANE and rq <= LANE

    w_in_bf = w_in.astype(BF16)
    wf = w_in[:, :, 8 * width:]
    wf_pad = jnp.pad(wf, ((0, 0), (0, 0), (0, LANE - N_HEADS))).astype(BF16)
    wft_pad = jnp.pad(jnp.swapaxes(wf, 1, 2), ((0, 0), (0, 2 * SUBLANE - N_HEADS), (0, 0))).astype(BF16)
    w_out_bf = w_out.astype(BF16)
    bf_row = b_forget.reshape(depth, 1, N_HEADS)
    bf_col = b_forget.reshape(depth, N_HEADS, 1)
    g_pre = norm_pre.reshape(depth, 1, d)
    g_post = norm_post.reshape(depth, 1, d)
    lq1 = lambda_q1.reshape(depth, 1, DIFF_QK_DIM)
    lk1 = lambda_k1.reshape(depth, 1, DIFF_QK_DIM)
    lq2 = lambda_q2.reshape(depth, 1, DIFF_QK_DIM)
    lk2 = lambda_k2.reshape(depth, 1, DIFF_QK_DIM)
    subln = diff_subln.reshape(depth, 1, HEAD_DIM)

    rows = -(-(batch + nb) // (2 * SUBLANE)) * (2 * SUBLANE)
    c_all = jnp.concatenate([c_prompt, c_sample, jnp.zeros((rows - batch - nb, d), F32)], axis=0)
    mod = _ada_call(c_all, w_ada, b_ada)

    r_i = jnp.arange(blk, dtype=jnp.int32)
    dist = r_i[:, None] - r_i[None, :]
    idx_p = jnp.concatenate([_t5_bucket(jnp.maximum(dist, 0)), _t5_bucket(dist + blk)], axis=0)
    bias_p = _bias_call(rel_bias, idx_p, LOG2E).reshape(N_HEADS, 2, blk, blk)
    tq = jnp.arange(2 * t_new, dtype=jnp.int32)[:, None] % t_new
    pos_last = jnp.arange(prow, dtype=jnp.int32)[None, :] // N_HEADS
    pos_new = jnp.arange(LANE, dtype=jnp.int32)[None, :] // N_HEADS
    idx_dec = jnp.concatenate([_t5_bucket(tq + page - pos_last), _t5_bucket(jnp.maximum(tq - pos_new, 0))], axis=1)
    bias_dec = _bias_call(rel_bias, idx_dec)
    bias_dec = jnp.transpose(bias_dec.reshape(N_HEADS, 2, t_new, prow + LANE), (1, 2, 0, 3)).reshape(2 * rq, prow + LANE)
    bias_last, bias_new = bias_dec[:, :prow], bias_dec[:, prow:]

    cdk = cache_diff_k.reshape(depth, n_pool, prow, HEAD_DIM)
    cdv = cache_diff_v.reshape(depth, n_pool, prow, HEAD_DIM)
    cfk = cache_fox_k.reshape(depth, n_pool, prow, HEAD_DIM)
    cfv = cache_fox_v.reshape(depth, n_pool, prow, HEAD_DIM)
    clf = jnp.swapaxes(cache_fox_logf, 2, 3)

    xp = x_prompt.reshape(m_p, d)
    xs = x_sample.reshape(m_s, d)
    tm_p = min(1024, seq)
    kv_p = [jnp.zeros((depth, m_p, N_HEADS, HEAD_DIM), F32) for _ in KV_GROUPS]
    kv_s = [jnp.zeros((depth, m_s, N_HEADS, HEAD_DIM), F32) for _ in KV_GROUPS]
    logf_p, logf_s = [], []
    for l in range(depth):
        mp = mod[l, :batch]
        ms = jnp.repeat(mod[l, batch:batch + nb], t_new, axis=0)
        shift_p, scale_p, gate_p = (mp[:, None, k * d:(k + 1) * d] for k in range(3))
        shift_s, scale_s, gate_s = (ms[None, :, k * d:(k + 1) * d] for k in range(3))

        zb, logf, logft, kv_p = _proj_call(xp, shift_p, scale_p, g_pre, w_in_bf, wf_pad, wft_pad, bf_row, bf_col,
                                           kv_p, l, tm_p, LOG2E, "prompt_in_proj")
        ck_rows, cum_cols = _cum_call(logft, batch)
        yd, yf = _attn_call(zb, bias_p, ck_rows, cum_cols, rel_bias, lq1, lk1, lq2, lk2, subln, l, batch, blk)
        xp = _out_call(yd, yf, w_out_bf, xp, gate_p, g_post, l, min(512, seq), "prompt_out_proj")
        logf_p.append(logf.reshape(batch, seq, N_HEADS))

        zs, logf_n, logft_n, kv_s = _proj_call(xs, shift_s, scale_s, g_pre, w_in_bf, wf_pad, wft_pad, bf_row, bf_col,
                                               kv_s, l, m_s, 1.0, "sample_in_proj")
        lfn = jnp.transpose(logft_n.reshape(N_HEADS, nb, t_new), (1, 0, 2))
        yds, yfs = _dec_call(page_table, rel_bias, zs.reshape(8, nb, rq, HEAD_DIM), lfn, bias_last, bias_new,
                             lq1, lk1, lq2, lk2, subln, cdk, cdv, cfk, cfv, clf, l, t_new)
        xs = _out_call(yds.reshape(m_s, width), yfs.reshape(m_s, width), w_out_bf, xs, gate_s, g_post,
                       l, m_s, "sample_out_proj")
        logf_s.append(logf_n.reshape(nb, t_new, N_HEADS))

    outs_p = [a.reshape(depth, batch, seq, N_HEADS, HEAD_DIM) for a in kv_p]
    outs_s = [a.reshape(depth, nb, t_new, N_HEADS, HEAD_DIM) for a in kv_s]
    return (xp.reshape(batch, seq, d), xs.reshape(nb, t_new, d), *outs_p, jnp.stack(logf_p, axis=0),
            *outs_s, jnp.stack(logf_s, axis=0))
```
